```python
import jax, jax.numpy as jnp
from jax import lax
import numpy as np

D_MODEL = 1024
BATCH = 4
SEQ = 8192
DEPTH = 2

HEAD_DIM = 64
ROPE_THETA = 10000.0
Q_BLOCK = 128
LN_EPS = 1e-5

NSA_HEADS = 8
NSA_KV_HEADS = 2
NSA_GROUP = NSA_HEADS // NSA_KV_HEADS
CMP_BLOCK = 32
CMP_STRIDE = 16
CMP_HIDDEN = 2 * HEAD_DIM
SLC_BLOCK = 64
SLC_TOPN = 16
WINDOW = 512
FORCE_BONUS = 1.0e4

MOBA_HEADS = 8
MOBA_BLOCK = 256
MOBA_TOPK = 3
MOBA_Q_BLOCK = 32

SB_HEADS = 16

N_GROUPS = 4
EXPERTS_PER_GROUP = 4
N_EXPERTS = N_GROUPS * EXPERTS_PER_GROUP
TOPK_IN_GROUP = 2
EXPERT_HIDDEN = 256

DEEPNORM_ALPHA = float((2 * DEPTH) ** 0.25)
DEEPNORM_BETA = float((8 * DEPTH) ** -0.25)

N_EVEN = (DEPTH + 1) // 2
N_ODD = DEPTH // 2

NSA_Q_W = NSA_HEADS * HEAD_DIM
NSA_KV_W = NSA_KV_HEADS * HEAD_DIM
NSA_GATE_W = 3 * NSA_HEADS
MOBA_W = MOBA_HEADS * HEAD_DIM
EVEN_WIDTHS = (NSA_Q_W, NSA_KV_W, NSA_KV_W, NSA_KV_W, NSA_KV_W, NSA_KV_W, NSA_KV_W,
               NSA_GATE_W, MOBA_W, MOBA_W, MOBA_W)
EVEN_IN_W = sum(EVEN_WIDTHS)
EVEN_SPLITS = tuple(int(v) for v in np.cumsum(EVEN_WIDTHS)[:-1])
EVEN_OUT_W = NSA_Q_W + MOBA_W
SB_W = SB_HEADS * HEAD_DIM

kernel_name = 'hybrid_nsa_moba_stickbreak_hmoe'


def layer_norm(x, g, b):
    xf = x.astype(jnp.float32)
    mu = jnp.mean(xf, axis=-1, keepdims=True)
    var = jnp.mean(jnp.square(xf - mu), axis=-1, keepdims=True)
    return ((xf - mu) * lax.rsqrt(var + LN_EPS) * g + b).astype(x.dtype)


def rope(x, positions):
    half = x.shape[-1] // 2
    inv_freq = ROPE_THETA ** (-jnp.arange(half, dtype=jnp.float32) / half)
    ang = positions.astype(jnp.float32)[..., None] * inv_freq
    cos = jnp.cos(ang)[:, :, None, :]
    sin = jnp.sin(ang)[:, :, None, :]
    x1 = x[..., :half].astype(jnp.float32)
    x2 = x[..., half:].astype(jnp.float32)
    return jnp.concatenate([x1 * cos - x2 * sin, x2 * cos + x1 * sin], axis=-1).astype(x.dtype)


def masked_softmax(s, mask):
    s = jnp.where(mask, s.astype(jnp.float32), -jnp.inf)
    m = jnp.max(s, axis=-1, keepdims=True)
    m = jnp.where(jnp.isfinite(m), m, 0.0)
    e = jnp.where(mask, jnp.exp(s - m), 0.0)
    return e / jnp.maximum(jnp.sum(e, axis=-1, keepdims=True), 1e-30)


def nsa_attention(q, k_cmp, v_cmp, k_slc, v_slc, k_win, v_win, gates,
                  cmp_pos_k, cmp_pos_v, cmp_w1_k, cmp_w2_k, cmp_w1_v, cmp_w2_v):
    B, S = q.shape[0], q.shape[1]
    scale = HEAD_DIM ** -0.5
    n_cmp = (S - CMP_BLOCK) // CMP_STRIDE + 1
    cmp_start = jnp.arange(n_cmp) * CMP_STRIDE
    cmp_end = cmp_start + CMP_BLOCK - 1
    tok_idx = cmp_start[:, None] + jnp.arange(CMP_BLOCK)[None, :]

    def compress(kv, pos_emb, w1, w2):
        blocks = kv[:, tok_idx] + pos_emb[None, None, :, None, :]
        flat = blocks.transpose(0, 1, 3, 2, 4).reshape(B, n_cmp, NSA_KV_HEADS, CMP_BLOCK * HEAD_DIM)
        return jax.nn.gelu(flat @ w1) @ w2

    kc = compress(k_cmp, cmp_pos_k, cmp_w1_k, cmp_w2_k)
    vc = compress(v_cmp, cmp_pos_v, cmp_w1_v, cmp_w2_v)

    n_slc = S // SLC_BLOCK
    slc_start = jnp.arange(n_slc) * SLC_BLOCK
    overlap = ((cmp_start[:, None] < slc_start[None, :] + SLC_BLOCK)
               & (cmp_end[:, None] >= slc_start[None, :])).astype(jnp.float32)
    top_n = min(SLC_TOPN, n_slc)
    kb = k_slc.reshape(B, n_slc, SLC_BLOCK, NSA_KV_HEADS, HEAD_DIM).transpose(0, 3, 1, 2, 4)
    vb = v_slc.reshape(B, n_slc, SLC_BLOCK, NSA_KV_HEADS, HEAD_DIM).transpose(0, 3, 1, 2, 4)
    pad = ((0, 0), (WINDOW, 0), (0, 0), (0, 0))
    kwp = jnp.pad(k_win, pad)
    vwp = jnp.pad(v_win, pad)
    bi = jnp.arange(B)[:, None, None, None]
    hi = jnp.arange(NSA_KV_HEADS)[None, :, None, None]
    blk_ids = jnp.arange(n_slc)

    def one_block(b):
        q0 = b * Q_BLOCK
        t = q0 + jnp.arange(Q_BLOCK)
        qb = lax.dynamic_slice_in_dim(q, q0, Q_BLOCK, axis=1).reshape(
            B, Q_BLOCK, NSA_KV_HEADS, NSA_GROUP, HEAD_DIM)
        gb = lax.dynamic_slice_in_dim(gates, q0, Q_BLOCK, axis=1).reshape(
            B, Q_BLOCK, NSA_KV_HEADS, NSA_GROUP, 3)
        s_c = jnp.einsum('bqkgd,bnkd->bkgqn', qb, kc) * scale
        p_c = masked_softmax(s_c, cmp_end[None, :] <= t[:, None])
        o_c = jnp.einsum('bkgqn,bnkd->bqkgd', p_c.astype(vc.dtype), vc)
        imp = jnp.einsum('bkgqn,nj->bkqj', p_c, overlap)
        cur = t // SLC_BLOCK
        forced = ((blk_ids[None, :] == 0) | (blk_ids[None, :] == cur[:, None])
                  | (blk_ids[None, :] == cur[:, None] - 1))
        valid = blk_ids[None, :] <= cur[:, None]
        score = jnp.where(forced, imp + FORCE_BONUS, jnp.where(valid, imp, -1.0))
        _, sel = lax.top_k(score, top_n)
        sel_ok = sel <= cur[None, None, :, None]
        ks = kb[bi, hi, sel]
        vs = vb[bi, hi, sel]
        kpos = sel[..., None] * SLC_BLOCK + jnp.arange(SLC_BLOCK)
        m_s = sel_ok[..., None] & (kpos <= t[None, None, :, None, None])
        n_keys = top_n * SLC_BLOCK
        s_s = jnp.einsum('bqkgd,bkqnld->bkgqnl', qb, ks).reshape(
            B, NSA_KV_HEADS, NSA_GROUP, Q_BLOCK, n_keys) * scale
        p_s = masked_softmax(s_s, m_s.reshape(B, NSA_KV_HEADS, Q_BLOCK, n_keys)[:, :, None])
        o_s = jnp.einsum('bkgqm,bkqmd->bqkgd', p_s.astype(vs.dtype),
                         vs.reshape(B, NSA_KV_HEADS, Q_BLOCK, n_keys, HEAD_DIM))
        kw = lax.dynamic_slice_in_dim(kwp, q0, WINDOW + Q_BLOCK, axis=1)
        vw = lax.dynamic_slice_in_dim(vwp, q0, WINDOW + Q_BLOCK, axis=1)
        wpos = q0 - WINDOW + jnp.arange(WINDOW + Q_BLOCK)
        dist = t[:, None] - wpos[None, :]
        m_w = (dist >= 0) & (dist < WINDOW) & (wpos[None, :] >= 0)
        s_w = jnp.einsum('bqkgd,bskd->bkgqs', qb, kw) * scale
        p_w = masked_softmax(s_w, m_w)
        o_w = jnp.einsum('bkgqs,bskd->bqkgd', p_w.astype(vw.dtype), vw)
        o = gb[..., 0:1] * o_c + gb[..., 1:2] * o_s + gb[..., 2:3] * o_w
        return o.reshape(B, Q_BLOCK, NSA_Q_W)

    out = lax.map(one_block, jnp.arange(S // Q_BLOCK))
    return out.transpose(1, 0, 2, 3).reshape(B, S, NSA_Q_W)


def moba_attention(q, k, v):
    B, S = q.shape[0], q.shape[1]
    scale = HEAD_DIM ** -0.5
    n_blk = -(-S // MOBA_BLOCK)
    pad = n_blk * MOBA_BLOCK - S
    kp = jnp.pad(k, ((0, 0), (0, pad), (0, 0), (0, 0)))
    vp = jnp.pad(v, ((0, 0), (0, pad), (0, 0), (0, 0)))
    kb = kp.reshape(B, n_blk, MOBA_BLOCK, MOBA_HEADS, HEAD_DIM).transpose(0, 3, 1, 2, 4)
    vb = vp.reshape(B, n_blk, MOBA_BLOCK, MOBA_HEADS, HEAD_DIM).transpose(0, 3, 1, 2, 4)
    k_mean = jnp.mean(kb.astype(jnp.float32), axis=3).astype(k.dtype)
    top_k = min(MOBA_TOPK, n_blk)
    bi = jnp.arange(B)[:, None, None, None]
    hi = jnp.arange(MOBA_HEADS)[None, :, None, None]
    blk_ids = jnp.arange(n_blk)
    n_sel = top_k * MOBA_BLOCK

    def one_block(b):
        q0 = b * MOBA_Q_BLOCK
        t = q0 + jnp.arange(MOBA_Q_BLOCK)
        cur = q0 // MOBA_BLOCK
        qb = lax.dynamic_slice_in_dim(q, q0, MOBA_Q_BLOCK, axis=1)
        gate = jnp.einsum('bqhd,bhnd->bhqn', qb, k_mean).astype(jnp.float32)
        gate = jnp.where(blk_ids < cur, gate, -jnp.inf)
        _, sel = lax.top_k(gate, top_k)
        sel_ok = sel < cur
        ks = kb[bi, hi, sel]
        vs = vb[bi, hi, sel]
        s_sel = jnp.einsum('bqhd,bhqnld->bhqnl', qb, ks).reshape(B, MOBA_HEADS, MOBA_Q_BLOCK, n_sel) * scale
        m_sel = jnp.broadcast_to(sel_ok[..., None], sel.shape + (MOBA_BLOCK,)).reshape(
            B, MOBA_HEADS, MOBA_Q_BLOCK, n_sel)
        k_own = lax.dynamic_index_in_dim(kb, cur, axis=2, keepdims=False)
        v_own = lax.dynamic_index_in_dim(vb, cur, axis=2, keepdims=False)
        s_own = jnp.einsum('bqhd,bhld->bhql', qb, k_own) * scale
        own_pos = cur * MOBA_BLOCK + jnp.arange(MOBA_BLOCK)
        m_own = jnp.broadcast_to(own_pos[None, :] <= t[:, None], (B, MOBA_HEADS, MOBA_Q_BLOCK, MOBA_BLOCK))
        p = masked_softmax(jnp.concatenate([s_sel, s_own], axis=-1),
                           jnp.concatenate([m_sel, m_own], axis=-1)).astype(v.dtype)
        o = (jnp.einsum('bhqm,bhqmd->bqhd', p[..., :n_sel],
                        vs.reshape(B, MOBA_HEADS, MOBA_Q_BLOCK, n_sel, HEAD_DIM))
             + jnp.einsum('bhql,bhld->bqhd', p[..., n_sel:], v_own))
        return o.reshape(B, MOBA_Q_BLOCK, MOBA_W)

    out = lax.map(one_block, jnp.arange(S // MOBA_Q_BLOCK))
    return out.transpose(1, 0, 2, 3).reshape(B, S, MOBA_W)


def stick_breaking_attention(q, k, v):
    B, S = q.shape[0], q.shape[1]
    scale = HEAD_DIM ** -0.5
    kpos = jnp.arange(S)

    def one_block(b):
        q0 = b * Q_BLOCK
        t = q0 + jnp.arange(Q_BLOCK)
        qb = lax.dynamic_slice_in_dim(q, q0, Q_BLOCK, axis=1)
        z = jnp.einsum('bqhd,bshd->bhqs', qb, k).astype(jnp.float32) * scale
        mask = kpos[None, :] < t[:, None]
        log_1m = jnp.where(mask, jax.nn.log_sigmoid(-z), 0.0)
        between = lax.cumsum(log_1m, axis=3, reverse=True) - log_1m
        w = jnp.where(mask, jnp.exp(jax.nn.log_sigmoid(z) + between), 0.0)
        o = jnp.einsum('bhqs,bshd->bqhd', w.astype(v.dtype), v)
        return o.reshape(B, Q_BLOCK, SB_W)

    out = lax.map(one_block, jnp.arange(S // Q_BLOCK))
    return out.transpose(1, 0, 2, 3).reshape(B, S, SB_W)


def nsa_moba_mixer(x, positions, w_in, w_out, cmp_pos_k, cmp_pos_v, cmp_w1_k, cmp_w2_k, cmp_w1_v, cmp_w2_v):
    B, S = x.shape[0], x.shape[1]
    proj = x @ w_in
    qa, kc, vc, ksl, vsl, kw, vw, ga, qm, km, vm = jnp.split(proj, EVEN_SPLITS, axis=-1)
    heads = lambda a, h: a.reshape(B, S, h, HEAD_DIM)
    qa = rope(heads(qa, NSA_HEADS), positions)
    kc = rope(heads(kc, NSA_KV_HEADS), positions)
    ksl = rope(heads(ksl, NSA_KV_HEADS), positions)
    kw = rope(heads(kw, NSA_KV_HEADS), positions)
    gates = jax.nn.sigmoid(ga.astype(jnp.float32)).astype(x.dtype).reshape(B, S, NSA_HEADS, 3)
    o_a = nsa_attention(qa, kc, heads(vc, NSA_KV_HEADS), ksl, heads(vsl, NSA_KV_HEADS),
                        kw, heads(vw, NSA_KV_HEADS), gates,
                        cmp_pos_k, cmp_pos_v, cmp_w1_k, cmp_w2_k, cmp_w1_v, cmp_w2_v)
    o_b = moba_attention(rope(heads(qm, MOBA_HEADS), positions),
                         rope(heads(km, MOBA_HEADS), positions), heads(vm, MOBA_HEADS))
    return jnp.concatenate([o_a, o_b], axis=-1) @ w_out


def stick_breaking_mixer(x, w_in, w_out):
    B, S = x.shape[0], x.shape[1]
    q, k, v = jnp.split(x @ w_in, 3, axis=-1)
    heads = lambda a: a.reshape(B, S, SB_HEADS, HEAD_DIM)
    return stick_breaking_attention(heads(q), heads(k), heads(v)) @ w_out


def hierarchical_moe(x, w_grp, b_grp, w_rt, b_rt, w_gate, w_up, w_down):
    p_grp = jax.nn.softmax((x @ w_grp + b_grp).astype(jnp.float32), axis=-1)
    g = jnp.argmax(p_grp, axis=-1)
    w_g = jnp.max(p_grp, axis=-1)
    logits_all = jnp.einsum('bsd,gde->bsge', x, w_rt) + b_rt
    logits = jnp.take_along_axis(logits_all, g[..., None, None], axis=2)[..., 0, :].astype(jnp.float32)
    top_val, top_idx = lax.top_k(logits, TOPK_IN_GROUP)
    w_e = jax.nn.softmax(top_val, axis=-1) * w_g[..., None]
    expert_id = g[..., None] * EXPERTS_PER_GROUP + top_idx
    combine = jnp.sum(jax.nn.one_hot(expert_id, N_EXPERTS, dtype=jnp.float32) * w_e[..., None], axis=2)

    def per_sequence(args):
        xs, cs = args
        h = jax.nn.silu(jnp.einsum('sd,edf->sef', xs, w_gate)) * jnp.einsum('sd,edf->sef', xs, w_up)
        return jnp.einsum('sef,efd->sd', h * cs[..., None].astype(h.dtype), w_down)

    return lax.map(per_sequence, (x, combine))


def _normal(key, shape, std):
    return jax.random.normal(key, shape, jnp.float32) * std


def setup_inputs(seed: int = 0) -> dict:
    key = jax.random.key(seed)
    ks = jax.random.split(key, 24)
    D = D_MODEL
    flat_cmp = CMP_BLOCK * HEAD_DIM
    return {
        'x': jax.random.normal(ks[0], (BATCH, SEQ, D), jnp.float32),
        'positions': jnp.broadcast_to(jnp.arange(SEQ, dtype=jnp.int32), (BATCH, SEQ)),
        'ab_w_in': _normal(ks[1], (N_EVEN, D, EVEN_IN_W), D ** -0.5),
        'ab_w_out': _normal(ks[2], (N_EVEN, EVEN_OUT_W, D), DEEPNORM_BETA * EVEN_OUT_W ** -0.5),
        'nsa_cmp_pos_k': _normal(ks[3], (N_EVEN, CMP_BLOCK, HEAD_DIM), 0.02),
        'nsa_cmp_pos_v': _normal(ks[4], (N_EVEN, CMP_BLOCK, HEAD_DIM), 0.02),
        'nsa_cmp_w1_k': _normal(ks[5], (N_EVEN, flat_cmp, CMP_HIDDEN), flat_cmp ** -0.5),
        'nsa_cmp_w2_k': _normal(ks[6], (N_EVEN, CMP_HIDDEN, HEAD_DIM), CMP_HIDDEN ** -0.5),
        'nsa_cmp_w1_v': _normal(ks[7], (N_EVEN, flat_cmp, CMP_HIDDEN), flat_cmp ** -0.5),
        'nsa_cmp_w2_v': _normal(ks[8], (N_EVEN, CMP_HIDDEN, HEAD_DIM), CMP_HIDDEN ** -0.5),
        'sb_w_in': _normal(ks[9], (N_ODD, D, 3 * SB_W), D ** -0.5),
        'sb_w_out': _normal(ks[10], (N_ODD, SB_W, D), DEEPNORM_BETA * SB_W ** -0.5),
        'ln_mix_g': 1.0 + _normal(ks[11], (DEPTH, D), 0.02),
        'ln_mix_b': _normal(ks[12], (DEPTH, D), 0.02),
        'ln_ffn_g': 1.0 + _normal(ks[13], (DEPTH, D), 0.02),
        'ln_ffn_b': _normal(ks[14], (DEPTH, D), 0.02),
        'moe_w_grp': _normal(ks[15], (DEPTH, D, N_GROUPS), D ** -0.5),
        'moe_b_grp': _normal(ks[16], (DEPTH, N_GROUPS), 0.01),
        'moe_w_rt': _normal(ks[17], (DEPTH, N_GROUPS, D, EXPERTS_PER_GROUP), D ** -0.5),
        'moe_b_rt': _normal(ks[18], (DEPTH, N_GROUPS, EXPERTS_PER_GROUP), 0.01),
        'moe_w_gate': _normal(ks[19], (DEPTH, N_EXPERTS, D, EXPERT_HIDDEN), D ** -0.5),
        'moe_w_up': _normal(ks[20], (DEPTH, N_EXPERTS, D, EXPERT_HIDDEN), D ** -0.5),
        'moe_w_down': _normal(ks[21], (DEPTH, N_EXPERTS, EXPERT_HIDDEN, D), DEEPNORM_BETA * EXPERT_HIDDEN ** -0.5),
    }


def reference(x, positions, ab_w_in, ab_w_out, nsa_cmp_pos_k, nsa_cmp_pos_v, nsa_cmp_w1_k, nsa_cmp_w2_k,
              nsa_cmp_w1_v, nsa_cmp_w2_v, sb_w_in, sb_w_out, ln_mix_g, ln_mix_b, ln_ffn_g, ln_ffn_b,
              moe_w_grp, moe_b_grp, moe_w_rt, moe_b_rt, moe_w_gate, moe_w_up, moe_w_down):
    for layer in range(DEPTH):
        i = layer // 2
        if layer % 2 == 0:
            mix = nsa_moba_mixer(x, positions, ab_w_in[i], ab_w_out[i], nsa_cmp_pos_k[i], nsa_cmp_pos_v[i],
                                 nsa_cmp_w1_k[i], nsa_cmp_w2_k[i], nsa_cmp_w1_v[i], nsa_cmp_w2_v[i])
        else:
            mix = stick_breaking_mixer(x, sb_w_in[i], sb_w_out[i])
        x = layer_norm(DEEPNORM_ALPHA * x + mix, ln_mix_g[layer], ln_mix_b[layer])
        ffn = hierarchical_moe(x, moe_w_grp[layer], moe_b_grp[layer], moe_w_rt[layer], moe_b_rt[layer],
                               moe_w_gate[layer], moe_w_up[layer], moe_w_down[layer])
        x = layer_norm(DEEPNORM_ALPHA * x + ffn, ln_ffn_g[layer], ln_ffn_b[layer])
    return x
```

```python
import functools
import math

import numpy as np
import jax
import jax.numpy as jnp
from jax import lax
from jax.experimental import pallas as pl
from jax.experimental.pallas import tpu as pltpu

F32 = jnp.float32
BF16 = jnp.bfloat16

LANES = 128
VMEM_LIMIT_BYTES = 56 * 1024 * 1024

HEAD_DIM = 64
ROPE_THETA = 10000.0
LN_EPS = 1e-5

NSA_HEADS = 8
NSA_KV_HEADS = 2
NSA_GROUP = NSA_HEADS // NSA_KV_HEADS
CMP_BLOCK = 32
CMP_STRIDE = 16
CMP_HIDDEN = 2 * HEAD_DIM
SLC_BLOCK = 64
SLC_TOPN = 16
WINDOW = 512
FORCE_BONUS = 1.0e4

MOBA_HEADS = 8
MOBA_BLOCK = 256
MOBA_TOPK = 3

SB_HEADS = 16

N_GROUPS = 4
EXPERTS_PER_GROUP = 4
N_EXPERTS = N_GROUPS * EXPERTS_PER_GROUP
EXPERT_HIDDEN = 256

DEPTH = 2
DEEPNORM_ALPHA = float((2 * DEPTH) ** 0.25)

NSA_Q_W = NSA_HEADS * HEAD_DIM
NSA_KV_W = NSA_KV_HEADS * HEAD_DIM
NSA_GATE_W = 3 * NSA_HEADS
MOBA_W = MOBA_HEADS * HEAD_DIM
SB_W = SB_HEADS * HEAD_DIM

SCALE = HEAD_DIM ** -0.5
MASK_NEG = -(2.0 ** 60)
SB_EXP_ZERO = -110.0

ROW_TILE = 512
NSA_Q_TILE = 128
NSA_SEL_K_TILE = 512
NSA_WIN_K_TILE = 256
MOBA_Q_TILE = 512
MOBA_K_TILE = 512
MOBA_AUG = 64
SB_TILE = 256


def _cparams(*sem):
    return pltpu.CompilerParams(dimension_semantics=sem, vmem_limit_bytes=VMEM_LIMIT_BYTES)


def _dot(a, b):
    return jnp.dot(a, b, preferred_element_type=F32)


def _dot_nt(a, b):
    return lax.dot_general(a, b, (((1,), (1,)), ((), ())), preferred_element_type=F32)


def _split2(x):
    hi = x.astype(BF16)
    lo = (x - hi.astype(F32)).astype(BF16)
    return hi, lo


def _rope_table_kernel(pos_ref, inv_ref, cos_ref, sin_ref):
    ang = pos_ref[...] * inv_ref[...]
    lane = lax.broadcasted_iota(jnp.int32, ang.shape, 1)
    sign = jnp.where((lane % HEAD_DIM) < HEAD_DIM // 2, -1.0, 1.0)
    cos_ref[...] = jnp.cos(ang)
    sin_ref[...] = jnp.sin(ang) * sign


def _rope_tables(pos_f32):
    T = pos_f32.shape[0]
    half = HEAD_DIM // 2
    inv = ROPE_THETA ** (-np.arange(half, dtype=np.float64) / half)
    inv_row = jnp.asarray(np.tile(inv, LANES // half)[None, :], F32)
    tm = ROW_TILE
    return pl.pallas_call(
        _rope_table_kernel,
        grid=(T // tm,),
        in_specs=[pl.BlockSpec((tm, 1), lambda i: (i, 0)),
                  pl.BlockSpec((1, LANES), lambda i: (0, 0))],
        out_specs=[pl.BlockSpec((tm, LANES), lambda i: (i, 0))] * 2,
        out_shape=[jax.ShapeDtypeStruct((T, LANES), F32)] * 2,
        compiler_params=_cparams("parallel"),
    )(pos_f32, inv_row)


def _proj_kernel(*refs, mode):
    if mode == "rope":
        x_ref, w_ref, cos_ref, sin_ref, o_ref = refs
    else:
        x_ref, w_ref, o_ref = refs
    acc = _dot(x_ref[...].astype(BF16), w_ref[...])
    if mode == "rope":
        cos = cos_ref[...]
        sin = sin_ref[...]
        lane = lax.broadcasted_iota(jnp.int32, cos.shape, 1)
        first_half = (lane % HEAD_DIM) < HEAD_DIM // 2
        for c in range(acc.shape[1] // LANES):
            a = acc[:, c * LANES:(c + 1) * LANES]
            swapped = jnp.where(first_half,
                                pltpu.roll(a, LANES - HEAD_DIM // 2, 1),
                                pltpu.roll(a, HEAD_DIM // 2, 1))
            o_ref[:, c * LANES:(c + 1) * LANES] = (a * cos + swapped * sin).astype(o_ref.dtype)
    elif mode == "sigmoid":
        o_ref[...] = (1.0 / (1.0 + jnp.exp(-acc))).astype(o_ref.dtype)
    else:
        o_ref[...] = acc.astype(o_ref.dtype)


def _project(x, w_bf16, mode, out_dtype, tables=None):
    T, D = x.shape
    N = w_bf16.shape[1]
    tm = ROW_TILE
    in_specs = [pl.BlockSpec((tm, D), lambda i: (i, 0)),
                pl.BlockSpec((D, N), lambda i: (0, 0))]
    args = [x, w_bf16]
    if mode == "rope":
        in_specs += [pl.BlockSpec((tm, LANES), lambda i: (i, 0))] * 2
        args += list(tables)
    return pl.pallas_call(
        functools.partial(_proj_kernel, mode=mode),
        grid=(T // tm,),
        in_specs=in_specs,
        out_specs=pl.BlockSpec((tm, N), lambda i: (i, 0)),
        out_shape=jax.ShapeDtypeStruct((T, N), out_dtype),
        compiler_params=_cparams("parallel"),
    )(*args)


def _compress_kernel(kv_ref, pos_ref, w1_ref, w2_ref, o_ref):
    kv = kv_ref[0].astype(F32)
    n = kv.shape[0]
    half = CMP_STRIDE * HEAD_DIM
    first = _dot((kv + pos_ref[0:1, :]).astype(BF16), w1_ref[0:half, :])
    second = _dot((kv + pos_ref[1:2, :]).astype(BF16), w1_ref[half:2 * half, :])
    h = first + pltpu.roll(second, n - 1, 0)
    g = 0.5 * h * (1.0 + jnp.tanh(math.sqrt(2.0 / math.pi) * (h + 0.044715 * (h * h * h))))
    o_ref[0] = _dot(g.astype(BF16), w2_ref[...]).astype(o_ref.dtype)


def _compress(kv, pos_emb, w1, w2):
    NB, S, _ = kv.shape
    n = S // CMP_STRIDE
    half = CMP_STRIDE * HEAD_DIM
    kvr = kv.reshape(NB, n, half)
    pos2 = pos_emb.reshape(2, half)
    return pl.pallas_call(
        _compress_kernel,
        grid=(NB,),
        in_specs=[pl.BlockSpec((1, n, half), lambda b: (b, 0, 0)),
                  pl.BlockSpec((2, half), lambda b: (0, 0)),
                  pl.BlockSpec((2 * half, CMP_HIDDEN), lambda b: (0, 0)),
                  pl.BlockSpec((CMP_HIDDEN, HEAD_DIM), lambda b: (0, 0))],
        out_specs=pl.BlockSpec((1, n, HEAD_DIM), lambda b: (b, 0, 0)),
        out_shape=jax.ShapeDtypeStruct((NB, n, HEAD_DIM), BF16),
        compiler_params=_cparams("parallel"),
    )(kvr, pos2, w1.astype(BF16), w2.astype(BF16))


def _first_max_pick(score, lane, width):
    m = jnp.max(score, axis=-1, keepdims=True)
    idx = jnp.min(jnp.where(score == m, lane, width), axis=-1, keepdims=True)
    return lane == idx


def _nsa_cmp_kernel(q_ref, kc_ref, vc_ref, oc_ref, qaug_ref, *, tq, n_slc, top_n):
    G = NSA_GROUP
    R = G * tq
    q0 = pl.program_id(1) * tq
    q = q_ref[0].reshape(R, HEAD_DIM)
    kc = kc_ref[0]
    vc = vc_ref[0]
    ncp = kc.shape[0]
    s = _dot_nt(q * SCALE, kc)
    t = q0 + (lax.broadcasted_iota(jnp.int32, (R, 1), 0) % tq)
    cmp_end = lax.broadcasted_iota(jnp.int32, (1, ncp), 1) * CMP_STRIDE + (CMP_BLOCK - 1)
    mask = cmp_end <= t
    s = jnp.where(mask, s, -jnp.inf)
    m = jnp.max(s, axis=-1, keepdims=True)
    m = jnp.where(m == -jnp.inf, 0.0, m)
    e = jnp.where(mask, jnp.exp(s - m), 0.0)
    p = e / jnp.maximum(jnp.sum(e, axis=-1, keepdims=True), 1e-30)
    oc_ref[0] = _dot(p.astype(BF16), vc).reshape(G, tq, HEAD_DIM)

    pg = jnp.sum(p.reshape(G, tq, ncp), axis=0)
    ci = lax.broadcasted_iota(jnp.int32, (ncp, n_slc), 0)
    sj = lax.broadcasted_iota(jnp.int32, (ncp, n_slc), 1)
    ratio = SLC_BLOCK // CMP_STRIDE
    overlap = ((ci * CMP_STRIDE < (sj + 1) * SLC_BLOCK)
               & (ci * CMP_STRIDE + CMP_BLOCK - 1 >= sj * SLC_BLOCK))
    overlap = jnp.where(overlap, 1.0, 0.0).astype(BF16)
    del ratio
    p_hi, p_lo = _split2(pg)
    imp = _dot(p_hi, overlap) + _dot(p_lo, overlap)

    blk = lax.broadcasted_iota(jnp.int32, (tq, n_slc), 1)
    cur = (q0 + lax.broadcasted_iota(jnp.int32, (tq, 1), 0)) // SLC_BLOCK
    forced = (blk == 0) | (blk == cur) | (blk == cur - 1)
    valid = blk <= cur
    score = jnp.where(forced, imp + FORCE_BONUS, jnp.where(valid, imp, -1.0))

    def pick(_, carry):
        score, sel = carry
        hit = _first_max_pick(score, blk, n_slc)
        return jnp.where(hit, -jnp.inf, score), jnp.where(hit, 1.0, sel)

    _, sel = lax.fori_loop(0, top_n, pick, (score, jnp.zeros((tq, n_slc), F32)))
    neg = jnp.where((sel > 0.0) & valid, 0.0, MASK_NEG).astype(BF16)

    qs = q_ref[0]
    for g in range(G):
        qaug_ref[0, g, :, 0:HEAD_DIM] = qs[g]
        qaug_ref[0, g, :, HEAD_DIM:2 * HEAD_DIM] = jnp.zeros((tq, HEAD_DIM), BF16)
        qaug_ref[0, g, :, 2 * HEAD_DIM:2 * HEAD_DIM + n_slc] = neg


def _nsa_compressed(q, kc, vc):
    NB, G, S, _ = q.shape
    ncp = kc.shape[1]
    n_slc = S // SLC_BLOCK
    C = 2 * HEAD_DIM + n_slc
    tq = NSA_Q_TILE
    kern = functools.partial(_nsa_cmp_kernel, tq=tq, n_slc=n_slc, top_n=min(SLC_TOPN, n_slc))
    return pl.pallas_call(
        kern,
        grid=(NB, S // tq),
        in_specs=[pl.BlockSpec((1, G, tq, HEAD_DIM), lambda b, i: (b, 0, i, 0)),
                  pl.BlockSpec((1, ncp, HEAD_DIM), lambda b, i: (b, 0, 0)),
                  pl.BlockSpec((1, ncp, HEAD_DIM), lambda b, i: (b, 0, 0))],
        out_specs=[pl.BlockSpec((1, G, tq, HEAD_DIM), lambda b, i: (b, 0, i, 0)),
                   pl.BlockSpec((1, G, tq, C), lambda b, i: (b, 0, i, 0))],
        out_shape=[jax.ShapeDtypeStruct((NB, G, S, HEAD_DIM), F32),
                   jax.ShapeDtypeStruct((NB, G, S, C), BF16)],
        compiler_params=_cparams("parallel", "parallel"),
    )(q, kc, vc)


def _flash_kernel(q_ref, k_ref, v_ref, o_ref, m_scr, l_scr, acc_scr, *, tq, tk, window):
    G = q_ref.shape[1]
    C = q_ref.shape[3]
    R = G * tq
    q0 = pl.program_id(1) * tq
    q = q_ref[0].reshape(R, C) * SCALE
    t = q0 + (lax.broadcasted_iota(jnp.int32, (R, 1), 0) % tq)

    m_scr[...] = jnp.full((R, 1), -jnp.inf, F32)
    l_scr[...] = jnp.zeros((R, 1), F32)
    acc_scr[...] = jnp.zeros((R, HEAD_DIM), F32)

    j_hi = (q0 + tq - 1) // tk
    j_lo = 0 if window is None else jnp.maximum(q0 - (window - 1), 0) // tk

    def body(j, carry):
        k0 = pl.multiple_of(j * tk, tk)
        k = k_ref[0, pl.ds(k0, tk), :]
        v = v_ref[0, pl.ds(k0, tk), :]
        s = _dot_nt(q, k)
        kpos = k0 + lax.broadcasted_iota(jnp.int32, (1, tk), 1)
        mask = kpos <= t
        if window is not None:
            mask = mask & (t - kpos < window)
        s = jnp.where(mask, s, -jnp.inf)
        m_old = m_scr[...]
        m_new = jnp.maximum(m_old, jnp.max(s, axis=-1, keepdims=True))
        m_safe = jnp.where(m_new == -jnp.inf, 0.0, m_new)
        alpha = jnp.exp(m_old - m_safe)
        p = jnp.exp(s - m_safe)
        l_scr[...] = alpha * l_scr[...] + jnp.sum(p, axis=-1, keepdims=True)
        acc_scr[...] = alpha * acc_scr[...] + _dot(p.astype(BF16), v)
        m_scr[...] = m_new
        return carry

    lax.fori_loop(j_lo, j_hi + 1, body, 0)
    o_ref[0] = (acc_scr[...] / l_scr[...]).reshape(G, tq, HEAD_DIM)


def _flash(q, k, v, *, tq, tk, window=None):
    NB, G, S, C = q.shape
    tk = min(tk, S)
    R = G * tq
    kern = functools.partial(_flash_kernel, tq=tq, tk=tk, window=window)
    return pl.pallas_call(
        kern,
        grid=(NB, S // tq),
        in_specs=[pl.BlockSpec((1, G, tq, C), lambda b, i: (b, 0, i, 0)),
                  pl.BlockSpec((1, S, C), lambda b, i: (b, 0, 0)),
                  pl.BlockSpec((1, S, HEAD_DIM), lambda b, i: (b, 0, 0))],
        out_specs=pl.BlockSpec((1, G, tq, HEAD_DIM), lambda b, i: (b, 0, i, 0)),
        out_shape=jax.ShapeDtypeStruct((NB, G, S, HEAD_DIM), F32),
        scratch_shapes=[pltpu.VMEM((R, 1), F32), pltpu.VMEM((R, 1), F32),
                        pltpu.VMEM((R, HEAD_DIM), F32)],
        compiler_params=_cparams("parallel", "parallel"),
    )(q, k, v)


def _nsa_combine_kernel(oc_ref, os_ref, ow_ref, g_ref, o_ref):
    g = g_ref[...]
    for h in range(NSA_HEADS):
        sl = slice(h * HEAD_DIM, (h + 1) * HEAD_DIM)
        o = (g[:, 3 * h:3 * h + 1] * oc_ref[:, sl]
             + g[:, 3 * h + 1:3 * h + 2] * os_ref[:, sl]
             + g[:, 3 * h + 2:3 * h + 3] * ow_ref[:, sl])
        o_ref[:, sl] = o.astype(o_ref.dtype)


def _nsa_combine(oc, os_, ow, gates):
    T = oc.shape[0]
    tm = ROW_TILE
    spec = pl.BlockSpec((tm, NSA_Q_W), lambda i: (i, 0))
    return pl.pallas_call(
        _nsa_combine_kernel,
        grid=(T // tm,),
        in_specs=[spec, spec, spec, pl.BlockSpec((tm, LANES), lambda i: (i, 0))],
        out_specs=spec,
        out_shape=jax.ShapeDtypeStruct((T, NSA_Q_W), BF16),
        compiler_params=_cparams("parallel"),
    )(oc, os_, ow, gates)


def _block_mean_kernel(k_ref, o_ref, *, n_blk):
    k = k_ref[0].astype(F32).reshape(n_blk, MOBA_BLOCK, HEAD_DIM)
    o_ref[0] = (jnp.sum(k, axis=1) * (1.0 / MOBA_BLOCK)).astype(o_ref.dtype)


def _block_mean(k):
    NB, S, _ = k.shape
    n_blk = S // MOBA_BLOCK
    return pl.pallas_call(
        functools.partial(_block_mean_kernel, n_blk=n_blk),
        grid=(NB,),
        in_specs=[pl.BlockSpec((1, S, HEAD_DIM), lambda b: (b, 0, 0))],
        out_specs=pl.BlockSpec((1, n_blk, HEAD_DIM), lambda b: (b, 0, 0)),
        out_shape=jax.ShapeDtypeStruct((NB, n_blk, HEAD_DIM), BF16),
        compiler_params=_cparams("parallel"),
    )(k)


def _moba_gate_kernel(q_ref, km_ref, qaug_ref, *, tq, n_blk, top_k):
    q0 = pl.program_id(1) * tq
    q = q_ref[0, 0]
    gate = _dot_nt(q, km_ref[0])
    blk = lax.broadcasted_iota(jnp.int32, (tq, n_blk), 1)
    cur = (q0 + lax.broadcasted_iota(jnp.int32, (tq, 1), 0)) // MOBA_BLOCK
    past = blk < cur
    gate = jnp.where(past, gate, -jnp.inf)
    sel = jnp.zeros((tq, n_blk), F32)
    taken = jnp.zeros((tq, n_blk), F32)
    for _ in range(top_k):
        hit = _first_max_pick(jnp.where(taken > 0.0, -jnp.inf, gate), blk, n_blk)
        hit = hit & (taken == 0.0)
        taken = jnp.where(hit, 1.0, taken)
        sel = jnp.where(hit & past, 1.0, sel)
    keep = (sel > 0.0) | (blk == cur)
    neg = jnp.where(keep, 0.0, MASK_NEG).astype(BF16)
    qaug_ref[0, 0, :, 0:HEAD_DIM] = q
    qaug_ref[0, 0, :, HEAD_DIM:HEAD_DIM + n_blk] = neg
    if n_blk < MOBA_AUG:
        qaug_ref[0, 0, :, HEAD_DIM + n_blk:] = jnp.zeros((tq, MOBA_AUG - n_blk), BF16)


def _moba_gate(q, kmean):
    NB, _, S, _ = q.shape
    n_blk = kmean.shape[1]
    tq = min(MOBA_Q_TILE, S)
    C = HEAD_DIM + MOBA_AUG
    kern = functools.partial(_moba_gate_kernel, tq=tq, n_blk=n_blk, top_k=min(MOBA_TOPK, n_blk))
    return pl.pallas_call(
        kern,
        grid=(NB, S // tq),
        in_specs=[pl.BlockSpec((1, 1, tq, HEAD_DIM), lambda b, i: (b, 0, i, 0)),
                  pl.BlockSpec((1, n_blk, HEAD_DIM), lambda b, i: (b, 0, 0))],
        out_specs=pl.BlockSpec((1, 1, tq, C), lambda b, i: (b, 0, i, 0)),
        out_shape=jax.ShapeDtypeStruct((NB, 1, S, C), BF16),
        compiler_params=_cparams("parallel", "parallel"),
    )(q, kmean)


def _sb_kernel(q_ref, k_ref, v_ref, o_ref, carry_scr, acc_scr, *, tile):
    qi = pl.program_id(1)
    q0 = qi * tile
    q = q_ref[0] * SCALE
    t = q0 + lax.broadcasted_iota(jnp.int32, (tile, 1), 0)
    later = (lax.broadcasted_iota(jnp.int32, (tile, tile), 0)
             > lax.broadcasted_iota(jnp.int32, (tile, tile), 1))
    later = jnp.where(later, 1.0, 0.0).astype(BF16)

    carry_scr[...] = jnp.zeros((tile, 1), F32)
    acc_scr[...] = jnp.zeros((tile, HEAD_DIM), F32)

    def cond(state):
        j, worst = state
        return (j >= 0) & (worst > SB_EXP_ZERO)

    def body(state):
        j, _ = state
        k0 = pl.multiple_of(j * tile, tile)
        k = k_ref[0, pl.ds(k0, tile), :]
        v = v_ref[0, pl.ds(k0, tile), :]
        z = _dot_nt(q, k)
        kpos = k0 + lax.broadcasted_iota(jnp.int32, (1, tile), 1)
        mask = kpos < t
        sp = jnp.maximum(z, 0.0) + jnp.log1p(jnp.exp(-jnp.abs(z)))
        log_1m = jnp.where(mask, -sp, 0.0)
        hi, lo = _split2(log_1m)
        between = _dot(hi, later) + _dot(lo, later) + carry_scr[...]
        w = jnp.where(mask, jnp.exp((z - sp) + between), 0.0)
        acc_scr[...] += _dot(w.astype(BF16), v)
        carry = carry_scr[...] + jnp.sum(log_1m, axis=-1, keepdims=True)
        carry_scr[...] = carry
        return j - 1, jnp.max(carry)

    lax.while_loop(cond, body, (qi, jnp.float32(0.0)))
    o_ref[0] = acc_scr[...]


def _stick_breaking(q, k, v):
    NB, S, _ = q.shape
    tile = min(SB_TILE, S)
    return pl.pallas_call(
        functools.partial(_sb_kernel, tile=tile),
        grid=(NB, S // tile),
        in_specs=[pl.BlockSpec((1, tile, HEAD_DIM), lambda b, i: (b, i, 0)),
                  pl.BlockSpec((1, S, HEAD_DIM), lambda b, i: (b, 0, 0)),
                  pl.BlockSpec((1, S, HEAD_DIM), lambda b, i: (b, 0, 0))],
        out_specs=pl.BlockSpec((1, tile, HEAD_DIM), lambda b, i: (b, i, 0)),
        out_shape=jax.ShapeDtypeStruct((NB, S, HEAD_DIM), F32),
        scratch_shapes=[pltpu.VMEM((tile, 1), F32), pltpu.VMEM((tile, HEAD_DIM), F32)],
        compiler_params=_cparams("parallel", "parallel"),
    )(q, k, v)


def _layer_norm(y, g, b):
    mu = jnp.mean(y, axis=-1, keepdims=True)
    d = y - mu
    var = jnp.mean(d * d, axis=-1, keepdims=True)
    return d * lax.rsqrt(var + LN_EPS) * g + b


def _out_ln_kernel(*refs, n_in):
    x_ref = refs[0]
    o_refs = refs[1:1 + n_in]
    w_refs = refs[1 + n_in:1 + 2 * n_in]
    g_ref, b_ref, y_ref = refs[1 + 2 * n_in:]
    mix = _dot(o_refs[0][...].astype(BF16), w_refs[0][...])
    for o_ref, w_ref in zip(o_refs[1:], w_refs[1:]):
        mix += _dot(o_ref[...].astype(BF16), w_ref[...])
    y_ref[...] = _layer_norm(DEEPNORM_ALPHA * x_ref[...] + mix, g_ref[...], b_ref[...])


def _out_ln(x, outs, ws, g, b):
    T, D = x.shape
    tm = ROW_TILE
    n_in = len(outs)
    in_specs = [pl.BlockSpec((tm, D), lambda i: (i, 0))]
    in_specs += [pl.BlockSpec((tm, o.shape[1]), lambda i: (i, 0)) for o in outs]
    in_specs += [pl.BlockSpec(w.shape, lambda i: (0, 0)) for w in ws]
    in_specs += [pl.BlockSpec((1, D), lambda i: (0, 0))] * 2
    return pl.pallas_call(
        functools.partial(_out_ln_kernel, n_in=n_in),
        grid=(T // tm,),
        in_specs=in_specs,
        out_specs=pl.BlockSpec((tm, D), lambda i: (i, 0)),
        out_shape=jax.ShapeDtypeStruct((T, D), F32),
        compiler_params=_cparams("parallel"),
    )(x, *outs, *ws, g.reshape(1, D), b.reshape(1, D))


def _route(x, wr_ref, br_ref):
    x1 = x.astype(BF16)
    r1 = x - x1.astype(F32)
    x2 = r1.astype(BF16)
    x3 = (r1 - x2.astype(F32)).astype(BF16)
    w1, w2, w3 = wr_ref[0], wr_ref[1], wr_ref[2]
    logits = (_dot(x3, w1) + _dot(x2, w2) + _dot(x1, w3)
              + _dot(x2, w1) + _dot(x1, w2) + _dot(x1, w1) + br_ref[...])
    lane = lax.broadcasted_iota(jnp.int32, logits.shape, 1)
    is_grp = lane < N_GROUPS
    lg = jnp.where(is_grp, logits, -jnp.inf)
    mg = jnp.max(lg, axis=-1, keepdims=True)
    gidx = jnp.min(jnp.where(lg == mg, lane, LANES), axis=-1, keepdims=True)
    w_g = 1.0 / jnp.sum(jnp.where(is_grp, jnp.exp(logits - mg), 0.0), axis=-1, keepdims=True)
    first = N_GROUPS + gidx * EXPERTS_PER_GROUP
    in_grp = (lane >= first) & (lane < first + EXPERTS_PER_GROUP)
    le = jnp.where(in_grp, logits, -jnp.inf)
    v1 = jnp.max(le, axis=-1, keepdims=True)
    i1 = jnp.min(jnp.where(le == v1, lane, LANES), axis=-1, keepdims=True)
    le2 = jnp.where(lane == i1, -jnp.inf, le)
    v2 = jnp.max(le2, axis=-1, keepdims=True)
    i2 = jnp.min(jnp.where(le2 == v2, lane, LANES), axis=-1, keepdims=True)
    e2 = jnp.exp(v2 - v1)
    den = 1.0 + e2
    return jnp.where(lane == i1, (1.0 / den) * w_g, jnp.where(lane == i2, (e2 / den) * w_g, 0.0))


def _moe_kernel(x_ref, wr_ref, br_ref, wg_ref, wu_ref, wd_ref, g_ref, b_ref, y_ref,
                comb_scr, acc_scr, xb_scr):
    e = pl.program_id(1)

    @pl.when(e == 0)
    def _():
        x = x_ref[...]
        comb_scr[...] = _route(x, wr_ref, br_ref)
        xb_scr[...] = x.astype(BF16)
        acc_scr[...] = jnp.zeros_like(acc_scr)

    xb = xb_scr[...]
    comb = comb_scr[...]
    lane = lax.broadcasted_iota(jnp.int32, comb.shape, 1)
    c = jnp.sum(jnp.where(lane == N_GROUPS + e, comb, 0.0), axis=-1, keepdims=True)
    gate = _dot(xb, wg_ref[0])
    up = _dot(xb, wu_ref[0])
    h = (gate * (1.0 / (1.0 + jnp.exp(-gate)))) * up
    acc_scr[...] += _dot((h * c).astype(BF16), wd_ref[0])

    @pl.when(e == N_EXPERTS - 1)
    def _():
        y_ref[...] = _layer_norm(DEEPNORM_ALPHA * x_ref[...] + acc_scr[...], g_ref[...], b_ref[...])


def _moe_ln(x, w_grp, b_grp, w_rt, b_rt, w_gate, w_up, w_down, g, b):
    T, D = x.shape
    tm = ROW_TILE
    wr = jnp.concatenate([w_grp, w_rt.transpose(1, 0, 2).reshape(D, N_EXPERTS)], axis=1)
    wr = jnp.pad(wr, ((0, 0), (0, LANES - wr.shape[1])))
    w1 = wr.astype(BF16)
    r1 = wr - w1.astype(F32)
    w2 = r1.astype(BF16)
    w3 = (r1 - w2.astype(F32)).astype(BF16)
    wr3 = jnp.stack([w1, w2, w3])
    br = jnp.pad(jnp.concatenate([b_grp, b_rt.reshape(N_EXPERTS)]), (0, LANES - N_GROUPS - N_EXPERTS))
    H = EXPERT_HIDDEN
    return pl.pallas_call(
        _moe_kernel,
        grid=(T // tm, N_EXPERTS),
        in_specs=[pl.BlockSpec((tm, D), lambda i, e: (i, 0)),
                  pl.BlockSpec((3, D, LANES), lambda i, e: (0, 0, 0)),
                  pl.BlockSpec((1, LANES), lambda i, e: (0, 0)),
                  pl.BlockSpec((1, D, H), lambda i, e: (e, 0, 0)),
                  pl.BlockSpec((1, D, H), lambda i, e: (e, 0, 0)),
                  pl.BlockSpec((1, H, D), lambda i, e: (e, 0, 0)),
                  pl.BlockSpec((1, D), lambda i, e: (0, 0)),
                  pl.BlockSpec((1, D), lambda i, e: (0, 0))],
        out_specs=pl.BlockSpec((tm, D), lambda i, e: (i, 0)),
        out_shape=jax.ShapeDtypeStruct((T, D), F32),
        scratch_shapes=[pltpu.VMEM((tm, LANES), F32), pltpu.VMEM((tm, D), F32),
                        pltpu.VMEM((tm, D), BF16)],
        compiler_params=_cparams("parallel", "arbitrary"),
    )(x, wr3, br.reshape(1, LANES), w_gate.astype(BF16), w_up.astype(BF16), w_down.astype(BF16),
      g.reshape(1, D), b.reshape(1, D))


def _heads(a, B, S, n):
    return a.reshape(B, S, n, HEAD_DIM).transpose(0, 2, 1, 3).reshape(B * n, S, HEAD_DIM)


def _unheads(a, B, S, n):
    return a.reshape(B, n, S, HEAD_DIM).transpose(0, 2, 1, 3).reshape(B * S, n * HEAD_DIM)


def _block_onehot(S, block, width):
    ids = np.arange(S)[:, None] // block == np.arange(width)[None, :]
    return jnp.asarray(ids, BF16)


def _nsa_moba_mixer(x, tables, B, S, w_in, cmp_pos_k, cmp_pos_v, cmp_w1_k, cmp_w2_k, cmp_w1_v, cmp_w2_v):
    T = B * S
    KV, G = NSA_KV_HEADS, NSA_GROUP
    widths = [NSA_Q_W] + [NSA_KV_W] * 6 + [NSA_GATE_W, MOBA_W, MOBA_W, MOBA_W]
    qa_w, kc_w, vc_w, ks_w, vs_w, kw_w, vw_w, ga_w, qm_w, km_w, vm_w = jnp.split(
        w_in, [int(v) for v in np.cumsum(widths)[:-1]], axis=1)
    w_rope = jnp.concatenate([qa_w, kc_w, ks_w, kw_w, qm_w, km_w], axis=1).astype(BF16)
    w_plain = jnp.concatenate([vc_w, vs_w, vw_w, vm_w], axis=1).astype(BF16)
    w_gate = jnp.pad(ga_w, ((0, 0), (0, LANES - NSA_GATE_W))).astype(BF16)

    roped = _project(x, w_rope, "rope", BF16, tables)
    plain = _project(x, w_plain, "plain", BF16)
    gates = _project(x, w_gate, "sigmoid", F32)

    o = 0
    qa = roped[:, o:o + NSA_Q_W]; o += NSA_Q_W
    kc = roped[:, o:o + NSA_KV_W]; o += NSA_KV_W
    ks = roped[:, o:o + NSA_KV_W]; o += NSA_KV_W
    kw = roped[:, o:o + NSA_KV_W]; o += NSA_KV_W
    qm = roped[:, o:o + MOBA_W]; o += MOBA_W
    km = roped[:, o:o + MOBA_W]
    vc = plain[:, 0:NSA_KV_W]
    vs = plain[:, NSA_KV_W:2 * NSA_KV_W]
    vw = plain[:, 2 * NSA_KV_W:3 * NSA_KV_W]
    vm = plain[:, 3 * NSA_KV_W:]

    q = _heads(qa, B, S, NSA_HEADS).reshape(B * KV, G, S, HEAD_DIM)
    kcc = _compress(_heads(kc, B, S, KV), cmp_pos_k, cmp_w1_k, cmp_w2_k)
    vcc = _compress(_heads(vc, B, S, KV), cmp_pos_v, cmp_w1_v, cmp_w2_v)
    o_c, q_aug = _nsa_compressed(q, kcc, vcc)
    n_slc = S // SLC_BLOCK
    k_sel = _heads(ks, B, S, KV)
    k_aug = jnp.concatenate(
        [k_sel, jnp.zeros((B * KV, S, HEAD_DIM), BF16),
         jnp.broadcast_to(_block_onehot(S, SLC_BLOCK, n_slc)[None], (B * KV, S, n_slc))], axis=-1)
    o_s = _flash(q_aug, k_aug, _heads(vs, B, S, KV), tq=NSA_Q_TILE, tk=NSA_SEL_K_TILE)
    o_w = _flash(q, _heads(kw, B, S, KV), _heads(vw, B, S, KV), tq=NSA_Q_TILE, tk=NSA_WIN_K_TILE,
                 window=WINDOW)
    flat = lambda a: _unheads(a.reshape(B * NSA_HEADS, S, HEAD_DIM), B, S, NSA_HEADS)
    o_a = _nsa_combine(flat(o_c), flat(o_s), flat(o_w), gates)

    H = MOBA_HEADS
    qh = _heads(qm, B, S, H)
    kh = _heads(km, B, S, H)
    n_blk = S // MOBA_BLOCK
    qm_aug = _moba_gate(qh.reshape(B * H, 1, S, HEAD_DIM), _block_mean(kh))
    km_aug = jnp.concatenate(
        [kh, jnp.broadcast_to(_block_onehot(S, MOBA_BLOCK, MOBA_AUG)[None], (B * H, S, MOBA_AUG))], axis=-1)
    o_b = _flash(qm_aug, km_aug, _heads(vm, B, S, H), tq=min(MOBA_Q_TILE, S), tk=MOBA_K_TILE)
    o_b = _unheads(o_b.reshape(B * H, S, HEAD_DIM), B, S, H)
    return o_a, o_b


def _sb_mixer(x, B, S, w_in):
    H = SB_HEADS
    qkv = _project(x, w_in.astype(BF16), "plain", BF16)
    q = _heads(qkv[:, 0:SB_W], B, S, H)
    k = _heads(qkv[:, SB_W:2 * SB_W], B, S, H)
    v = _heads(qkv[:, 2 * SB_W:], B, S, H)
    return _unheads(_stick_breaking(q, k, v), B, S, H)


def kernel(x, positions, ab_w_in, ab_w_out, nsa_cmp_pos_k, nsa_cmp_pos_v, nsa_cmp_w1_k, nsa_cmp_w2_k,
           nsa_cmp_w1_v, nsa_cmp_w2_v, sb_w_in, sb_w_out, ln_mix_g, ln_mix_b, ln_ffn_g, ln_ffn_b,
           moe_w_grp, moe_b_grp, moe_w_rt, moe_b_rt, moe_w_gate, moe_w_up, moe_w_down):
    B, S, D = x.shape
    T = B * S
    assert S % ROW_TILE == 0 and S % MOBA_BLOCK == 0 and S // MOBA_BLOCK <= MOBA_AUG
    h = x.reshape(T, D)
    tables = _rope_tables(positions.reshape(T, 1).astype(F32))
    n_layers = ln_mix_g.shape[0]
    for layer in range(n_layers):
        i = layer // 2
        if layer % 2 == 0:
            o_a, o_b = _nsa_moba_mixer(h, tables, B, S, ab_w_in[i], nsa_cmp_pos_k[i], nsa_cmp_pos_v[i],
                                       nsa_cmp_w1_k[i], nsa_cmp_w2_k[i], nsa_cmp_w1_v[i], nsa_cmp_w2_v[i])
            w_out = ab_w_out[i].astype(BF16)
            h = _out_ln(h, [o_a, o_b], [w_out[:NSA_Q_W], w_out[NSA_Q_W:]], ln_mix_g[layer], ln_mix_b[layer])
        else:
            o = _sb_mixer(h, B, S, sb_w_in[i])
            h = _out_ln(h, [o], [sb_w_out[i].astype(BF16)], ln_mix_g[layer], ln_mix_b[layer])
        h = _moe_ln(h, moe_w_grp[layer], moe_b_grp[layer], moe_w_rt[layer], moe_b_rt[layer],
                    moe_w_gate[layer], moe_w_up[layer], moe_w_down[layer], ln_ffn_g[layer], ln_ffn_b[layer])
    return h.reshape(B, S, D)
```

```python
import functools
import math

import numpy as np
import jax
import jax.numpy as jnp
from jax import lax
from jax.experimental import pallas as pl
from jax.experimental.pallas import tpu as pltpu

F32 = jnp.float32
BF16 = jnp.bfloat16

LANES = 128
VMEM_LIMIT_BYTES = 56 * 1024 * 1024

HEAD_DIM = 64
ROPE_THETA = 10000.0
LN_EPS = 1e-5

NSA_HEADS = 8
NSA_KV_HEADS = 2
NSA_GROUP = NSA_HEADS // NSA_KV_HEADS
CMP_BLOCK = 32
CMP_STRIDE = 16
CMP_HIDDEN = 2 * HEAD_DIM
SLC_BLOCK = 64
SLC_TOPN = 16
WINDOW = 512
FORCE_BONUS = 1.0e4

MOBA_HEADS = 8
MOBA_BLOCK = 256
MOBA_TOPK = 3

SB_HEADS = 16

N_GROUPS = 4
EXPERTS_PER_GROUP = 4
N_EXPERTS = N_GROUPS * EXPERTS_PER_GROUP
EXPERT_HIDDEN = 256

DEPTH = 2
DEEPNORM_ALPHA = float((2 * DEPTH) ** 0.25)

NSA_Q_W = NSA_HEADS * HEAD_DIM
NSA_KV_W = NSA_KV_HEADS * HEAD_DIM
NSA_GATE_W = 3 * NSA_HEADS
MOBA_W = MOBA_HEADS * HEAD_DIM
SB_W = SB_HEADS * HEAD_DIM

SCALE = HEAD_DIM ** -0.5
LOG2_E = 1.0 / math.log(2.0)
BF16_SUBLANES = 16
V_ROWS = -(-(HEAD_DIM + 1) // BF16_SUBLANES) * BF16_SUBLANES
FLASH_SPLIT = 2
MASK_NEG = -(2.0 ** 60)
SB_EXP_ZERO = -110.0

ROW_TILE = 512
NSA_Q_TILE = 128
NSA_FLASH_Q_TILE = 256
NSA_SEL_K_TILE = 512
NSA_WIN_K_TILE = 256
MOBA_Q_TILE = 512
MOBA_FLASH_Q_TILE = 1024
MOBA_K_TILE = 512
MOBA_AUG = 64
SB_TILE = 256
SB_HEADS_PER_STEP = 4


def _cparams(*sem):
    return pltpu.CompilerParams(dimension_semantics=sem, vmem_limit_bytes=VMEM_LIMIT_BYTES)


def _dot(a, b):
    return jnp.dot(a, b, preferred_element_type=F32)


def _dot_nt(a, b):
    return lax.dot_general(a, b, (((1,), (1,)), ((), ())), preferred_element_type=F32)


def _split2(x):
    hi = x.astype(BF16)
    lo = (x - hi.astype(F32)).astype(BF16)
    return hi, lo


def _rope_table_kernel(pos_ref, inv_ref, cos_ref, sin_ref):
    ang = pos_ref[...] * inv_ref[...]
    lane = lax.broadcasted_iota(jnp.int32, ang.shape, 1)
    sign = jnp.where((lane % HEAD_DIM) < HEAD_DIM // 2, -1.0, 1.0)
    cos_ref[...] = jnp.cos(ang)
    sin_ref[...] = jnp.sin(ang) * sign


def _rope_tables(pos_f32):
    T = pos_f32.shape[0]
    half = HEAD_DIM // 2
    inv = ROPE_THETA ** (-np.arange(half, dtype=np.float64) / half)
    inv_row = jnp.asarray(np.tile(inv, LANES // half)[None, :], F32)
    tm = ROW_TILE
    return pl.pallas_call(
        _rope_table_kernel,
        grid=(T // tm,),
        in_specs=[pl.BlockSpec((tm, 1), lambda i: (i, 0)),
                  pl.BlockSpec((1, LANES), lambda i: (0, 0))],
        out_specs=[pl.BlockSpec((tm, LANES), lambda i: (i, 0))] * 2,
        out_shape=[jax.ShapeDtypeStruct((T, LANES), F32)] * 2,
        compiler_params=_cparams("parallel"),
    )(pos_f32, inv_row)


def _proj_kernel(*refs, mode):
    if mode == "rope":
        x_ref, w_ref, cos_ref, sin_ref, o_ref = refs
    else:
        x_ref, w_ref, o_ref = refs
    acc = _dot(x_ref[...].astype(BF16), w_ref[...])
    if mode == "rope":
        cos = cos_ref[...]
        sin = sin_ref[...]
        lane = lax.broadcasted_iota(jnp.int32, cos.shape, 1)
        first_half = (lane % HEAD_DIM) < HEAD_DIM // 2
        for c in range(acc.shape[1] // LANES):
            a = acc[:, c * LANES:(c + 1) * LANES]
            swapped = jnp.where(first_half,
                                pltpu.roll(a, LANES - HEAD_DIM // 2, 1),
                                pltpu.roll(a, HEAD_DIM // 2, 1))
            o_ref[:, c * LANES:(c + 1) * LANES] = (a * cos + swapped * sin).astype(o_ref.dtype)
    elif mode == "sigmoid":
        o_ref[...] = (1.0 / (1.0 + jnp.exp(-acc))).astype(o_ref.dtype)
    else:
        o_ref[...] = acc.astype(o_ref.dtype)


def _project(x, w_bf16, mode, out_dtype, tables=None):
    T, D = x.shape
    N = w_bf16.shape[1]
    tm = ROW_TILE
    in_specs = [pl.BlockSpec((tm, D), lambda i: (i, 0)),
                pl.BlockSpec((D, N), lambda i: (0, 0))]
    args = [x, w_bf16]
    if mode == "rope":
        in_specs += [pl.BlockSpec((tm, LANES), lambda i: (i, 0))] * 2
        args += list(tables)
    return pl.pallas_call(
        functools.partial(_proj_kernel, mode=mode),
        grid=(T // tm,),
        in_specs=in_specs,
        out_specs=pl.BlockSpec((tm, N), lambda i: (i, 0)),
        out_shape=jax.ShapeDtypeStruct((T, N), out_dtype),
        compiler_params=_cparams("parallel"),
    )(*args)


def _compress_kernel(kv_ref, pos_ref, w1_ref, w2_ref, o_ref):
    kv = kv_ref[0].astype(F32)
    n = kv.shape[0]
    half = CMP_STRIDE * HEAD_DIM
    first = _dot((kv + pos_ref[0:1, :]).astype(BF16), w1_ref[0:half, :])
    second = _dot((kv + pos_ref[1:2, :]).astype(BF16), w1_ref[half:2 * half, :])
    h = first + pltpu.roll(second, n - 1, 0)
    g = 0.5 * h * (1.0 + jnp.tanh(math.sqrt(2.0 / math.pi) * (h + 0.044715 * (h * h * h))))
    o_ref[0] = _dot(g.astype(BF16), w2_ref[...]).astype(o_ref.dtype)


def _compress(kv, pos_emb, w1, w2):
    NB, S, _ = kv.shape
    n = S // CMP_STRIDE
    half = CMP_STRIDE * HEAD_DIM
    kvr = kv.reshape(NB, n, half)
    pos2 = pos_emb.reshape(2, half)
    return pl.pallas_call(
        _compress_kernel,
        grid=(NB,),
        in_specs=[pl.BlockSpec((1, n, half), lambda b: (b, 0, 0)),
                  pl.BlockSpec((2, half), lambda b: (0, 0)),
                  pl.BlockSpec((2 * half, CMP_HIDDEN), lambda b: (0, 0)),
                  pl.BlockSpec((CMP_HIDDEN, HEAD_DIM), lambda b: (0, 0))],
        out_specs=pl.BlockSpec((1, n, HEAD_DIM), lambda b: (b, 0, 0)),
        out_shape=jax.ShapeDtypeStruct((NB, n, HEAD_DIM), BF16),
        compiler_params=_cparams("parallel"),
    )(kvr, pos2, w1.astype(BF16), w2.astype(BF16))


def _first_max_pick(score, lane, width):
    m = jnp.max(score, axis=-1, keepdims=True)
    idx = jnp.min(jnp.where(score == m, lane, width), axis=-1, keepdims=True)
    return lane == idx


def _nsa_cmp_kernel(q_ref, kc_ref, vc_ref, oc_ref, qaug_ref, *, tq, n_slc, top_n):
    G = NSA_GROUP
    R = G * tq
    q0 = pl.program_id(1) * tq
    q = q_ref[0].reshape(R, HEAD_DIM)
    kc = kc_ref[0]
    vc = vc_ref[0]
    ncp = kc.shape[0]
    s = _dot_nt(q * SCALE, kc)
    t = q0 + (lax.broadcasted_iota(jnp.int32, (R, 1), 0) % tq)
    cmp_end = lax.broadcasted_iota(jnp.int32, (1, ncp), 1) * CMP_STRIDE + (CMP_BLOCK - 1)
    mask = cmp_end <= t
    s = jnp.where(mask, s, -jnp.inf)
    m = jnp.max(s, axis=-1, keepdims=True)
    m = jnp.where(m == -jnp.inf, 0.0, m)
    e = jnp.where(mask, jnp.exp(s - m), 0.0)
    p = e / jnp.maximum(jnp.sum(e, axis=-1, keepdims=True), 1e-30)
    oc_ref[0] = _dot(p.astype(BF16), vc).reshape(G, tq, HEAD_DIM)

    pg = jnp.sum(p.reshape(G, tq, ncp), axis=0)
    ci = lax.broadcasted_iota(jnp.int32, (ncp, n_slc), 0)
    sj = lax.broadcasted_iota(jnp.int32, (ncp, n_slc), 1)
    ratio = SLC_BLOCK // CMP_STRIDE
    overlap = ((ci * CMP_STRIDE < (sj + 1) * SLC_BLOCK)
               & (ci * CMP_STRIDE + CMP_BLOCK - 1 >= sj * SLC_BLOCK))
    overlap = jnp.where(overlap, 1.0, 0.0).astype(BF16)
    del ratio
    p_hi, p_lo = _split2(pg)
    imp = _dot(p_hi, overlap) + _dot(p_lo, overlap)

    blk = lax.broadcasted_iota(jnp.int32, (tq, n_slc), 1)
    cur = (q0 + lax.broadcasted_iota(jnp.int32, (tq, 1), 0)) // SLC_BLOCK
    forced = (blk == 0) | (blk == cur) | (blk == cur - 1)
    valid = blk <= cur
    score = jnp.where(forced, imp + FORCE_BONUS, jnp.where(valid, imp, -1.0))

    def pick(_, carry):
        score, sel = carry
        hit = _first_max_pick(score, blk, n_slc)
        return jnp.where(hit, -jnp.inf, score), jnp.where(hit, 1.0, sel)

    _, sel = lax.fori_loop(0, top_n, pick, (score, jnp.zeros((tq, n_slc), F32)))
    neg = jnp.where((sel > 0.0) & valid, 0.0, MASK_NEG).astype(BF16)

    qs = q_ref[0]
    for g in range(G):
        qaug_ref[0, g, :, 0:HEAD_DIM] = qs[g]
        qaug_ref[0, g, :, HEAD_DIM:2 * HEAD_DIM] = jnp.zeros((tq, HEAD_DIM), BF16)
        qaug_ref[0, g, :, 2 * HEAD_DIM:2 * HEAD_DIM + n_slc] = neg


def _nsa_compressed(q, kc, vc):
    NB, G, S, _ = q.shape
    ncp = kc.shape[1]
    n_slc = S // SLC_BLOCK
    C = 2 * HEAD_DIM + n_slc
    tq = NSA_Q_TILE
    kern = functools.partial(_nsa_cmp_kernel, tq=tq, n_slc=n_slc, top_n=min(SLC_TOPN, n_slc))
    return pl.pallas_call(
        kern,
        grid=(NB, S // tq),
        in_specs=[pl.BlockSpec((1, G, tq, HEAD_DIM), lambda b, i: (b, 0, i, 0)),
                  pl.BlockSpec((1, ncp, HEAD_DIM), lambda b, i: (b, 0, 0)),
                  pl.BlockSpec((1, ncp, HEAD_DIM), lambda b, i: (b, 0, 0))],
        out_specs=[pl.BlockSpec((1, G, tq, HEAD_DIM), lambda b, i: (b, 0, i, 0)),
                   pl.BlockSpec((1, G, tq, C), lambda b, i: (b, 0, i, 0))],
        out_shape=[jax.ShapeDtypeStruct((NB, G, S, HEAD_DIM), F32),
                   jax.ShapeDtypeStruct((NB, G, S, C), BF16)],
        compiler_params=_cparams("parallel", "parallel"),
    )(q, kc, vc)


def _flash_kernel(qt_ref, k_ref, vt_ref, o_ref, *scratch, tq, tk, window, n_split):
    G = o_ref.shape[1]
    R = G * tq
    W = R // n_split
    VR = vt_ref.shape[1]
    q0 = pl.program_id(1) * tq
    q_scrs, s_scrs, p_scrs, acc_scrs = (scratch[i * n_split:(i + 1) * n_split] for i in range(4))
    t_all = q0 + (lax.broadcasted_iota(jnp.int32, (1, R), 1) % tq)
    for h in range(n_split):
        q_scrs[h][...] = (qt_ref[0, 0, :, h * W:(h + 1) * W].astype(F32) * (SCALE * LOG2_E)).astype(BF16)
        acc_scrs[h][...] = jnp.zeros((VR, W), F32)

    def group(h, vt_tile, k0, m, masked):
        s_scr, p_scr, acc_scr = s_scrs[h], p_scrs[h], acc_scrs[h]
        t_row = t_all[:, h * W:(h + 1) * W]

        def load(r0, rows):
            s = s_scr[pl.ds(r0, rows), :]
            if masked:
                kpos = k0 + r0 + lax.broadcasted_iota(jnp.int32, (rows, 1), 0)
                if window is not None:
                    s = jnp.where(t_row - kpos < window, s, -jnp.inf)
                s = jnp.where(kpos <= t_row, s, -jnp.inf)
            return s

        def col_max(i, m8):
            return jnp.maximum(m8, load(pl.multiple_of(i * 8, 8), 8))

        m8 = lax.fori_loop(0, tk // 8, col_max, jnp.full((8, W), -jnp.inf, F32), unroll=True)
        m_new = jnp.maximum(m, jnp.max(m8, axis=0, keepdims=True))
        m_safe = jnp.where(m_new == -jnp.inf, 0.0, m_new)
        alpha = jnp.exp2(m - m_safe)

        def probs(i, carry):
            r0 = pl.multiple_of(i * 16, 16)
            p_scr[pl.ds(r0, 16), :] = jnp.exp2(load(r0, 16) - m_safe).astype(BF16)
            return carry

        lax.fori_loop(0, tk // 16, probs, 0, unroll=True)
        acc_scr[...] = alpha * acc_scr[...] + _dot(vt_tile, p_scr[...])
        return m_new

    def tile(j, ms, masked):
        k0 = pl.multiple_of(j * tk, tk)
        k_tile = k_ref[0, pl.ds(k0, tk), :]
        vt_tile = vt_ref[0, :, pl.ds(k0, tk)]
        for h in range(n_split):
            s_scrs[h][...] = _dot(k_tile, q_scrs[h][...])
        return tuple(group(h, vt_tile, k0, ms[h], masked) for h in range(n_split))

    ms = tuple(jnp.full((1, W), -jnp.inf, F32) for _ in range(n_split))
    j_hi = (q0 + tq - 1) // tk
    if window is None:
        j_diag = q0 // tk
        ms = lax.fori_loop(0, j_diag, lambda j, ms: tile(j, ms, False), ms)
        lax.fori_loop(j_diag, j_hi + 1, lambda j, ms: tile(j, ms, True), ms)
    else:
        j_lo = jnp.maximum(q0 - (window - 1), 0) // tk
        lax.fori_loop(j_lo, j_hi + 1, lambda j, ms: tile(j, ms, True), ms)
    gpg = G // n_split if G >= n_split else 1
    for h in range(n_split):
        acc = acc_scrs[h][...]
        out = acc[0:HEAD_DIM, :] / acc[HEAD_DIM:HEAD_DIM + 1, :]
        if G >= n_split:
            for g in range(gpg):
                o_ref[0, h * gpg + g] = out[:, g * tq:(g + 1) * tq]
        else:
            o_ref[0, 0, :, h * W:(h + 1) * W] = out


def _flash(q, k, vt, *, tq, tk, window=None):
    NB, G, S, C = q.shape
    tk = min(tk, S)
    R = G * tq
    ones = jnp.concatenate([jnp.ones((NB, 1, S), BF16), jnp.zeros((NB, V_ROWS - HEAD_DIM - 1, S), BF16)], axis=1)
    vt = jnp.concatenate([vt, ones], axis=1)
    qt = q.reshape(NB, G, S // tq, tq, C).transpose(0, 2, 4, 1, 3).reshape(NB, S // tq, C, R)
    ns = FLASH_SPLIT
    W = R // ns
    kern = functools.partial(_flash_kernel, tq=tq, tk=tk, window=window, n_split=ns)
    return pl.pallas_call(
        kern,
        grid=(NB, S // tq),
        in_specs=[pl.BlockSpec((1, 1, C, R), lambda b, i: (b, i, 0, 0)),
                  pl.BlockSpec((1, S, C), lambda b, i: (b, 0, 0)),
                  pl.BlockSpec((1, V_ROWS, S), lambda b, i: (b, 0, 0))],
        out_specs=pl.BlockSpec((1, G, HEAD_DIM, tq), lambda b, i: (b, 0, 0, i)),
        out_shape=jax.ShapeDtypeStruct((NB, G, HEAD_DIM, S), F32),
        scratch_shapes=([pltpu.VMEM((C, W), BF16)] * ns + [pltpu.VMEM((tk, W), F32)] * ns
                        + [pltpu.VMEM((tk, W), BF16)] * ns + [pltpu.VMEM((V_ROWS, W), F32)] * ns),
        compiler_params=_cparams("parallel", "parallel"),
    )(qt, k, vt)


def _nsa_combine_kernel(oc_ref, os_ref, ow_ref, g_ref, o_ref):
    g = g_ref[...]
    for h in range(NSA_HEADS):
        sl = slice(h * HEAD_DIM, (h + 1) * HEAD_DIM)
        o = (g[:, 3 * h:3 * h + 1] * oc_ref[:, sl]
             + g[:, 3 * h + 1:3 * h + 2] * os_ref[:, sl]
             + g[:, 3 * h + 2:3 * h + 3] * ow_ref[:, sl])
        o_ref[:, sl] = o.astype(o_ref.dtype)


def _nsa_combine(oc, os_, ow, gates):
    T = oc.shape[0]
    tm = ROW_TILE
    spec = pl.BlockSpec((tm, NSA_Q_W), lambda i: (i, 0))
    return pl.pallas_call(
        _nsa_combine_kernel,
        grid=(T // tm,),
        in_specs=[spec, spec, spec, pl.BlockSpec((tm, LANES), lambda i: (i, 0))],
        out_specs=spec,
        out_shape=jax.ShapeDtypeStruct((T, NSA_Q_W), BF16),
        compiler_params=_cparams("parallel"),
    )(oc, os_, ow, gates)


def _block_mean_kernel(k_ref, o_ref, *, n_blk):
    k = k_ref[0].astype(F32).reshape(n_blk, MOBA_BLOCK, HEAD_DIM)
    o_ref[0] = (jnp.sum(k, axis=1) * (1.0 / MOBA_BLOCK)).astype(o_ref.dtype)


def _block_mean(k):
    NB, S, _ = k.shape
    n_blk = S // MOBA_BLOCK
    return pl.pallas_call(
        functools.partial(_block_mean_kernel, n_blk=n_blk),
        grid=(NB,),
        in_specs=[pl.BlockSpec((1, S, HEAD_DIM), lambda b: (b, 0, 0))],
        out_specs=pl.BlockSpec((1, n_blk, HEAD_DIM), lambda b: (b, 0, 0)),
        out_shape=jax.ShapeDtypeStruct((NB, n_blk, HEAD_DIM), BF16),
        compiler_params=_cparams("parallel"),
    )(k)


def _moba_gate_kernel(q_ref, km_ref, qaug_ref, *, tq, n_blk, top_k):
    q0 = pl.program_id(1) * tq
    q = q_ref[0, 0]
    gate = _dot_nt(q, km_ref[0])
    blk = lax.broadcasted_iota(jnp.int32, (tq, n_blk), 1)
    cur = (q0 + lax.broadcasted_iota(jnp.int32, (tq, 1), 0)) // MOBA_BLOCK
    past = blk < cur
    gate = jnp.where(past, gate, -jnp.inf)
    sel = jnp.zeros((tq, n_blk), F32)
    taken = jnp.zeros((tq, n_blk), F32)
    for _ in range(top_k):
        hit = _first_max_pick(jnp.where(taken > 0.0, -jnp.inf, gate), blk, n_blk)
        hit = hit & (taken == 0.0)
        taken = jnp.where(hit, 1.0, taken)
        sel = jnp.where(hit & past, 1.0, sel)
    keep = (sel > 0.0) | (blk == cur)
    neg = jnp.where(keep, 0.0, MASK_NEG).astype(BF16)
    qaug_ref[0, 0, :, 0:HEAD_DIM] = q
    qaug_ref[0, 0, :, HEAD_DIM:HEAD_DIM + n_blk] = neg
    if n_blk < MOBA_AUG:
        qaug_ref[0, 0, :, HEAD_DIM + n_blk:] = jnp.zeros((tq, MOBA_AUG - n_blk), BF16)


def _moba_gate(q, kmean):
    NB, _, S, _ = q.shape
    n_blk = kmean.shape[1]
    tq = min(MOBA_Q_TILE, S)
    C = HEAD_DIM + MOBA_AUG
    kern = functools.partial(_moba_gate_kernel, tq=tq, n_blk=n_blk, top_k=min(MOBA_TOPK, n_blk))
    return pl.pallas_call(
        kern,
        grid=(NB, S // tq),
        in_specs=[pl.BlockSpec((1, 1, tq, HEAD_DIM), lambda b, i: (b, 0, i, 0)),
                  pl.BlockSpec((1, n_blk, HEAD_DIM), lambda b, i: (b, 0, 0))],
        out_specs=pl.BlockSpec((1, 1, tq, C), lambda b, i: (b, 0, i, 0)),
        out_shape=jax.ShapeDtypeStruct((NB, 1, S, C), BF16),
        compiler_params=_cparams("parallel", "parallel"),
    )(q, kmean)


def _sb_kernel(qt_ref, k_ref, vt_ref, o_ref, *acc_scrs, tile, hb):
    qi = pl.program_id(1)
    q0 = qi * tile
    t_row = q0 + lax.broadcasted_iota(jnp.int32, (1, tile), 1)
    later = (lax.broadcasted_iota(jnp.int32, (tile, tile), 1)
             > lax.broadcasted_iota(jnp.int32, (tile, tile), 0))
    later = jnp.where(later, 1.0, 0.0).astype(BF16)
    for acc in acc_scrs:
        acc[...] = jnp.zeros((HEAD_DIM, tile), F32)

    def step(j, carries, masked):
        k0 = pl.multiple_of(j * tile, tile)
        zs = [_dot(k_ref[h, pl.ds(k0, tile), :], qt_ref[h, 0] * SCALE) for h in range(hb)]
        if masked:
            mask = (k0 + lax.broadcasted_iota(jnp.int32, (tile, 1), 0)) < t_row
        sps = [jnp.maximum(z, 0.0) + jnp.log(1.0 + jnp.exp(-jnp.abs(z))) for z in zs]
        log_1ms = [jnp.where(mask, -sp, 0.0) if masked else -sp for sp in sps]
        parts = [_split2(x) for x in log_1ms]
        betweens = [_dot(later, hi) + _dot(later, lo) + c for (hi, lo), c in zip(parts, carries)]
        new_carries = []
        for h in range(hb):
            w = jnp.exp((zs[h] - sps[h]) + betweens[h])
            if masked:
                w = jnp.where(mask, w, 0.0)
            acc_scrs[h][...] += _dot(vt_ref[h, :, pl.ds(k0, tile)], w.astype(BF16))
            new_carries.append(carries[h] + jnp.sum(log_1ms[h], axis=0, keepdims=True))
        worst = new_carries[0]
        for c in new_carries[1:]:
            worst = jnp.maximum(worst, c)
        return tuple(new_carries), jnp.max(worst)

    carries, worst = step(qi, tuple(jnp.zeros((1, tile), F32) for _ in range(hb)), True)

    def cond(state):
        j, _, worst = state
        return (j >= 0) & (worst > SB_EXP_ZERO)

    def body(state):
        j, carries, _ = state
        carries, worst = step(j, carries, False)
        return j - 1, carries, worst

    lax.while_loop(cond, body, (qi - 1, carries, worst))
    for h in range(hb):
        o_ref[h] = acc_scrs[h][...]


def _stick_breaking(q, k, vt):
    NB, S, _ = q.shape
    tile = min(SB_TILE, S)
    hb = SB_HEADS_PER_STEP
    qt = q.reshape(NB, S // tile, tile, HEAD_DIM).transpose(0, 1, 3, 2)
    return pl.pallas_call(
        functools.partial(_sb_kernel, tile=tile, hb=hb),
        grid=(NB // hb, S // tile),
        in_specs=[pl.BlockSpec((hb, 1, HEAD_DIM, tile), lambda b, i: (b, i, 0, 0)),
                  pl.BlockSpec((hb, S, HEAD_DIM), lambda b, i: (b, 0, 0)),
                  pl.BlockSpec((hb, HEAD_DIM, S), lambda b, i: (b, 0, 0))],
        out_specs=pl.BlockSpec((hb, HEAD_DIM, tile), lambda b, i: (b, 0, i)),
        out_shape=jax.ShapeDtypeStruct((NB, HEAD_DIM, S), F32),
        scratch_shapes=[pltpu.VMEM((HEAD_DIM, tile), F32)] * hb,
        compiler_params=_cparams("parallel", "parallel"),
    )(qt, k, vt)


def _layer_norm(y, g, b):
    mu = jnp.mean(y, axis=-1, keepdims=True)
    d = y - mu
    var = jnp.mean(d * d, axis=-1, keepdims=True)
    return d * lax.rsqrt(var + LN_EPS) * g + b


def _out_ln_kernel(*refs, n_in):
    x_ref = refs[0]
    o_refs = refs[1:1 + n_in]
    w_refs = refs[1 + n_in:1 + 2 * n_in]
    g_ref, b_ref, y_ref = refs[1 + 2 * n_in:]
    mix = _dot(o_refs[0][...].astype(BF16), w_refs[0][...])
    for o_ref, w_ref in zip(o_refs[1:], w_refs[1:]):
        mix += _dot(o_ref[...].astype(BF16), w_ref[...])
    y_ref[...] = _layer_norm(DEEPNORM_ALPHA * x_ref[...] + mix, g_ref[...], b_ref[...])


def _out_ln(x, outs, ws, g, b):
    T, D = x.shape
    tm = ROW_TILE
    n_in = len(outs)
    in_specs = [pl.BlockSpec((tm, D), lambda i: (i, 0))]
    in_specs += [pl.BlockSpec((tm, o.shape[1]), lambda i: (i, 0)) for o in outs]
    in_specs += [pl.BlockSpec(w.shape, lambda i: (0, 0)) for w in ws]
    in_specs += [pl.BlockSpec((1, D), lambda i: (0, 0))] * 2
    return pl.pallas_call(
        functools.partial(_out_ln_kernel, n_in=n_in),
        grid=(T // tm,),
        in_specs=in_specs,
        out_specs=pl.BlockSpec((tm, D), lambda i: (i, 0)),
        out_shape=jax.ShapeDtypeStruct((T, D), F32),
        compiler_params=_cparams("parallel"),
    )(x, *outs, *ws, g.reshape(1, D), b.reshape(1, D))


def _route(x, wr_ref, br_ref):
    x1 = x.astype(BF16)
    r1 = x - x1.astype(F32)
    x2 = r1.astype(BF16)
    x3 = (r1 - x2.astype(F32)).astype(BF16)
    w1, w2, w3 = wr_ref[0], wr_ref[1], wr_ref[2]
    logits = (_dot(x3, w1) + _dot(x2, w2) + _dot(x1, w3)
              + _dot(x2, w1) + _dot(x1, w2) + _dot(x1, w1) + br_ref[...])
    lane = lax.broadcasted_iota(jnp.int32, logits.shape, 1)
    is_grp = lane < N_GROUPS
    lg = jnp.where(is_grp, logits, -jnp.inf)
    mg = jnp.max(lg, axis=-1, keepdims=True)
    gidx = jnp.min(jnp.where(lg == mg, lane, LANES), axis=-1, keepdims=True)
    w_g = 1.0 / jnp.sum(jnp.where(is_grp, jnp.exp(logits - mg), 0.0), axis=-1, keepdims=True)
    first = N_GROUPS + gidx * EXPERTS_PER_GROUP
    in_grp = (lane >= first) & (lane < first + EXPERTS_PER_GROUP)
    le = jnp.where(in_grp, logits, -jnp.inf)
    v1 = jnp.max(le, axis=-1, keepdims=True)
    i1 = jnp.min(jnp.where(le == v1, lane, LANES), axis=-1, keepdims=True)
    le2 = jnp.where(lane == i1, -jnp.inf, le)
    v2 = jnp.max(le2, axis=-1, keepdims=True)
    i2 = jnp.min(jnp.where(le2 == v2, lane, LANES), axis=-1, keepdims=True)
    e2 = jnp.exp(v2 - v1)
    den = 1.0 + e2
    return jnp.where(lane == i1, (1.0 / den) * w_g, jnp.where(lane == i2, (e2 / den) * w_g, 0.0))


def _moe_kernel(x_ref, wr_ref, br_ref, wg_ref, wu_ref, wd_ref, g_ref, b_ref, y_ref,
                comb_scr, acc_scr, xb_scr):
    e = pl.program_id(1)

    @pl.when(e == 0)
    def _():
        x = x_ref[...]
        comb_scr[...] = _route(x, wr_ref, br_ref)
        xb_scr[...] = x.astype(BF16)
        acc_scr[...] = jnp.zeros_like(acc_scr)

    xb = xb_scr[...]
    comb = comb_scr[...]
    lane = lax.broadcasted_iota(jnp.int32, comb.shape, 1)
    c = jnp.sum(jnp.where(lane == N_GROUPS + e, comb, 0.0), axis=-1, keepdims=True)
    gate = _dot(xb, wg_ref[0])
    up = _dot(xb, wu_ref[0])
    h = (gate * (1.0 / (1.0 + jnp.exp(-gate)))) * up
    acc_scr[...] += _dot((h * c).astype(BF16), wd_ref[0])

    @pl.when(e == N_EXPERTS - 1)
    def _():
        y_ref[...] = _layer_norm(DEEPNORM_ALPHA * x_ref[...] + acc_scr[...], g_ref[...], b_ref[...])


def _moe_ln(x, w_grp, b_grp, w_rt, b_rt, w_gate, w_up, w_down, g, b):
    T, D = x.shape
    tm = ROW_TILE
    wr = jnp.concatenate([w_grp, w_rt.transpose(1, 0, 2).reshape(D, N_EXPERTS)], axis=1)
    wr = jnp.pad(wr, ((0, 0), (0, LANES - wr.shape[1])))
    w1 = wr.astype(BF16)
    r1 = wr - w1.astype(F32)
    w2 = r1.astype(BF16)
    w3 = (r1 - w2.astype(F32)).astype(BF16)
    wr3 = jnp.stack([w1, w2, w3])
    br = jnp.pad(jnp.concatenate([b_grp, b_rt.reshape(N_EXPERTS)]), (0, LANES - N_GROUPS - N_EXPERTS))
    H = EXPERT_HIDDEN
    return pl.pallas_call(
        _moe_kernel,
        grid=(T // tm, N_EXPERTS),
        in_specs=[pl.BlockSpec((tm, D), lambda i, e: (i, 0)),
                  pl.BlockSpec((3, D, LANES), lambda i, e: (0, 0, 0)),
                  pl.BlockSpec((1, LANES), lambda i, e: (0, 0)),
                  pl.BlockSpec((1, D, H), lambda i, e: (e, 0, 0)),
                  pl.BlockSpec((1, D, H), lambda i, e: (e, 0, 0)),
                  pl.BlockSpec((1, H, D), lambda i, e: (e, 0, 0)),
                  pl.BlockSpec((1, D), lambda i, e: (0, 0)),
                  pl.BlockSpec((1, D), lambda i, e: (0, 0))],
        out_specs=pl.BlockSpec((tm, D), lambda i, e: (i, 0)),
        out_shape=jax.ShapeDtypeStruct((T, D), F32),
        scratch_shapes=[pltpu.VMEM((tm, LANES), F32), pltpu.VMEM((tm, D), F32),
                        pltpu.VMEM((tm, D), BF16)],
        compiler_params=_cparams("parallel", "arbitrary"),
    )(x, wr3, br.reshape(1, LANES), w_gate.astype(BF16), w_up.astype(BF16), w_down.astype(BF16),
      g.reshape(1, D), b.reshape(1, D))


def _heads(a, B, S, n):
    return a.reshape(B, S, n, HEAD_DIM).transpose(0, 2, 1, 3).reshape(B * n, S, HEAD_DIM)


def _unheads(a, B, S, n):
    return a.reshape(B, n, S, HEAD_DIM).transpose(0, 2, 1, 3).reshape(B * S, n * HEAD_DIM)


def _heads_t(a, B, S, n):
    return a.reshape(B, S, n, HEAD_DIM).transpose(0, 2, 3, 1).reshape(B * n, HEAD_DIM, S)


def _unheads_t(a, B, S, n):
    return a.reshape(B, n, HEAD_DIM, S).transpose(0, 3, 1, 2).reshape(B * S, n * HEAD_DIM)


def _block_onehot(S, block, width):
    ids = np.arange(S)[:, None] // block == np.arange(width)[None, :]
    return jnp.asarray(ids, BF16)


def _nsa_moba_mixer(x, tables, B, S, w_in, cmp_pos_k, cmp_pos_v, cmp_w1_k, cmp_w2_k, cmp_w1_v, cmp_w2_v):
    T = B * S
    KV, G = NSA_KV_HEADS, NSA_GROUP
    widths = [NSA_Q_W] + [NSA_KV_W] * 6 + [NSA_GATE_W, MOBA_W, MOBA_W, MOBA_W]
    qa_w, kc_w, vc_w, ks_w, vs_w, kw_w, vw_w, ga_w, qm_w, km_w, vm_w = jnp.split(
        w_in, [int(v) for v in np.cumsum(widths)[:-1]], axis=1)
    w_rope = jnp.concatenate([qa_w, kc_w, ks_w, kw_w, qm_w, km_w], axis=1).astype(BF16)
    w_plain = jnp.concatenate([vc_w, vs_w, vw_w, vm_w], axis=1).astype(BF16)
    w_gate = jnp.pad(ga_w, ((0, 0), (0, LANES - NSA_GATE_W))).astype(BF16)

    roped = _project(x, w_rope, "rope", BF16, tables)
    plain = _project(x, w_plain, "plain", BF16)
    gates = _project(x, w_gate, "sigmoid", F32)

    o = 0
    qa = roped[:, o:o + NSA_Q_W]; o += NSA_Q_W
    kc = roped[:, o:o + NSA_KV_W]; o += NSA_KV_W
    ks = roped[:, o:o + NSA_KV_W]; o += NSA_KV_W
    kw = roped[:, o:o + NSA_KV_W]; o += NSA_KV_W
    qm = roped[:, o:o + MOBA_W]; o += MOBA_W
    km = roped[:, o:o + MOBA_W]
    vc = plain[:, 0:NSA_KV_W]
    vs = plain[:, NSA_KV_W:2 * NSA_KV_W]
    vw = plain[:, 2 * NSA_KV_W:3 * NSA_KV_W]
    vm = plain[:, 3 * NSA_KV_W:]

    q = _heads(qa, B, S, NSA_HEADS).reshape(B * KV, G, S, HEAD_DIM)
    kcc = _compress(_heads(kc, B, S, KV), cmp_pos_k, cmp_w1_k, cmp_w2_k)
    vcc = _compress(_heads(vc, B, S, KV), cmp_pos_v, cmp_w1_v, cmp_w2_v)
    o_c, q_aug = _nsa_compressed(q, kcc, vcc)
    n_slc = S // SLC_BLOCK
    k_sel = _heads(ks, B, S, KV)
    k_aug = jnp.concatenate(
        [k_sel, jnp.zeros((B * KV, S, HEAD_DIM), BF16),
         jnp.broadcast_to(_block_onehot(S, SLC_BLOCK, n_slc)[None], (B * KV, S, n_slc))], axis=-1)
    o_s = _flash(q_aug, k_aug, _heads_t(vs, B, S, KV), tq=NSA_FLASH_Q_TILE, tk=NSA_SEL_K_TILE)
    o_w = _flash(q, _heads(kw, B, S, KV), _heads_t(vw, B, S, KV), tq=NSA_FLASH_Q_TILE, tk=NSA_WIN_K_TILE,
                 window=WINDOW)
    flat = lambda a: _unheads(a.reshape(B * NSA_HEADS, S, HEAD_DIM), B, S, NSA_HEADS)
    flat_t = lambda a: _unheads_t(a.reshape(B * NSA_HEADS, HEAD_DIM, S), B, S, NSA_HEADS)
    o_a = _nsa_combine(flat(o_c), flat_t(o_s), flat_t(o_w), gates)

    H = MOBA_HEADS
    qh = _heads(qm, B, S, H)
    kh = _heads(km, B, S, H)
    n_blk = S // MOBA_BLOCK
    qm_aug = _moba_gate(qh.reshape(B * H, 1, S, HEAD_DIM), _block_mean(kh))
    km_aug = jnp.concatenate(
        [kh, jnp.broadcast_to(_block_onehot(S, MOBA_BLOCK, MOBA_AUG)[None], (B * H, S, MOBA_AUG))], axis=-1)
    o_b = _flash(qm_aug, km_aug, _heads_t(vm, B, S, H), tq=min(MOBA_FLASH_Q_TILE, S), tk=MOBA_K_TILE)
    o_b = _unheads_t(o_b.reshape(B * H, HEAD_DIM, S), B, S, H)
    return o_a, o_b


def _sb_mixer(x, B, S, w_in):
    H = SB_HEADS
    qkv = _project(x, w_in.astype(BF16), "plain", BF16)
    q = _heads(qkv[:, 0:SB_W], B, S, H)
    k = _heads(qkv[:, SB_W:2 * SB_W], B, S, H)
    vt = _heads_t(qkv[:, 2 * SB_W:], B, S, H)
    return _unheads_t(_stick_breaking(q, k, vt), B, S, H)


def kernel(x, positions, ab_w_in, ab_w_out, nsa_cmp_pos_k, nsa_cmp_pos_v, nsa_cmp_w1_k, nsa_cmp_w2_k,
           nsa_cmp_w1_v, nsa_cmp_w2_v, sb_w_in, sb_w_out, ln_mix_g, ln_mix_b, ln_ffn_g, ln_ffn_b,
           moe_w_grp, moe_b_grp, moe_w_rt, moe_b_rt, moe_w_gate, moe_w_up, moe_w_down):
    B, S, D = x.shape
    T = B * S
    assert S % ROW_TILE == 0 and S % MOBA_BLOCK == 0 and S // MOBA_BLOCK <= MOBA_AUG
    h = x.reshape(T, D)
    tables = _rope_tables(positions.reshape(T, 1).astype(F32))
    n_layers = ln_mix_g.shape[0]
    for layer in range(n_layers):
        i = layer // 2
        if layer % 2 == 0:
            o_a, o_b = _nsa_moba_mixer(h, tables, B, S, ab_w_in[i], nsa_cmp_pos_k[i], nsa_cmp_pos_v[i],
                                       nsa_cmp_w1_k[i], nsa_cmp_w2_k[i], nsa_cmp_w1_v[i], nsa_cmp_w2_v[i])
            w_out = ab_w_out[i].astype(BF16)
            h = _out_ln(h, [o_a, o_b], [w_out[:NSA_Q_W], w_out[NSA_Q_W:]], ln_mix_g[layer], ln_mix_b[layer])
        else:
            o = _sb_mixer(h, B, S, sb_w_in[i])
            h = _out_ln(h, [o], [sb_w_out[i].astype(BF16)], ln_mix_g[layer], ln_mix_b[layer])
        h = _moe_ln(h, moe_w_grp[layer], moe_b_grp[layer], moe_w_rt[layer], moe_b_rt[layer],
                    moe_w_gate[layer], moe_w_up[layer], moe_w_down[layer], ln_ffn_g[layer], ln_ffn_b[layer])
    return h.reshape(B, S, D)
```

```python
import functools
import math

import numpy as np
import jax
import jax.numpy as jnp
from jax import lax
from jax.experimental import pallas as pl
from jax.experimental.pallas import tpu as pltpu

F32 = jnp.float32
BF16 = jnp.bfloat16

LANES = 128
VMEM_LIMIT_BYTES = 56 * 1024 * 1024

HEAD_DIM = 64
ROPE_THETA = 10000.0
LN_EPS = 1e-5

NSA_HEADS = 8
NSA_KV_HEADS = 2
NSA_GROUP = NSA_HEADS // NSA_KV_HEADS
CMP_BLOCK = 32
CMP_STRIDE = 16
CMP_HIDDEN = 2 * HEAD_DIM
SLC_BLOCK = 64
SLC_TOPN = 16
WINDOW = 512
FORCE_BONUS = 1.0e4

MOBA_HEADS = 8
MOBA_BLOCK = 256
MOBA_TOPK = 3

SB_HEADS = 16

N_GROUPS = 4
EXPERTS_PER_GROUP = 4
N_EXPERTS = N_GROUPS * EXPERTS_PER_GROUP
EXPERT_HIDDEN = 256

DEPTH = 2
DEEPNORM_ALPHA = float((2 * DEPTH) ** 0.25)

NSA_Q_W = NSA_HEADS * HEAD_DIM
NSA_KV_W = NSA_KV_HEADS * HEAD_DIM
NSA_GATE_W = 3 * NSA_HEADS
MOBA_W = MOBA_HEADS * HEAD_DIM
SB_W = SB_HEADS * HEAD_DIM

SCALE = HEAD_DIM ** -0.5
LOG2_E = 1.0 / math.log(2.0)
BF16_SUBLANES = 16
V_ROWS = -(-(HEAD_DIM + 1) // BF16_SUBLANES) * BF16_SUBLANES
FLASH_SPLIT = 2
MASK_NEG = -(2.0 ** 60)
SB_EXP_ZERO = -110.0

ROW_TILE = 512
NSA_Q_TILE = 128
NSA_FLASH_Q_TILE = 256
NSA_SEL_K_TILE = 512
NSA_WIN_K_TILE = 256
MOBA_Q_TILE = 512
MOBA_FLASH_Q_TILE = 1024
MOBA_K_TILE = 512
MOBA_AUG = 64
SB_TILE = 256
SB_HEADS_PER_STEP = 4


def _cparams(*sem):
    return pltpu.CompilerParams(dimension_semantics=sem, vmem_limit_bytes=VMEM_LIMIT_BYTES)


def _dot(a, b):
    return jnp.dot(a, b, preferred_element_type=F32)


def _dot_nt(a, b):
    return lax.dot_general(a, b, (((1,), (1,)), ((), ())), preferred_element_type=F32)


def _split2(x):
    hi = x.astype(BF16)
    lo = (x - hi.astype(F32)).astype(BF16)
    return hi, lo


def _rope_table_kernel(pos_ref, inv_ref, cos_ref, sin_ref):
    ang = pos_ref[...] * inv_ref[...]
    lane = lax.broadcasted_iota(jnp.int32, ang.shape, 1)
    sign = jnp.where((lane % HEAD_DIM) < HEAD_DIM // 2, -1.0, 1.0)
    cos_ref[...] = jnp.cos(ang)
    sin_ref[...] = jnp.sin(ang) * sign


def _rope_tables(pos_f32):
    T = pos_f32.shape[0]
    half = HEAD_DIM // 2
    inv = ROPE_THETA ** (-np.arange(half, dtype=np.float64) / half)
    inv_row = jnp.asarray(np.tile(inv, LANES // half)[None, :], F32)
    tm = ROW_TILE
    return pl.pallas_call(
        _rope_table_kernel,
        grid=(T // tm,),
        in_specs=[pl.BlockSpec((tm, 1), lambda i: (i, 0)),
                  pl.BlockSpec((1, LANES), lambda i: (0, 0))],
        out_specs=[pl.BlockSpec((tm, LANES), lambda i: (i, 0))] * 2,
        out_shape=[jax.ShapeDtypeStruct((T, LANES), F32)] * 2,
        compiler_params=_cparams("parallel"),
    )(pos_f32, inv_row)


def _proj_kernel(*refs, mode):
    if mode == "rope":
        x_ref, w_ref, cos_ref, sin_ref, o_ref = refs
    else:
        x_ref, w_ref, o_ref = refs
    acc = _dot(x_ref[...].astype(BF16), w_ref[...])
    if mode == "rope":
        cos = cos_ref[...]
        sin = sin_ref[...]
        lane = lax.broadcasted_iota(jnp.int32, cos.shape, 1)
        first_half = (lane % HEAD_DIM) < HEAD_DIM // 2
        for c in range(acc.shape[1] // LANES):
            a = acc[:, c * LANES:(c + 1) * LANES]
            swapped = jnp.where(first_half,
                                pltpu.roll(a, LANES - HEAD_DIM // 2, 1),
                                pltpu.roll(a, HEAD_DIM // 2, 1))
            o_ref[:, c * LANES:(c + 1) * LANES] = (a * cos + swapped * sin).astype(o_ref.dtype)
    elif mode == "sigmoid":
        o_ref[...] = (1.0 / (1.0 + jnp.exp(-acc))).astype(o_ref.dtype)
    else:
        o_ref[...] = acc.astype(o_ref.dtype)


def _project(x, w_bf16, mode, out_dtype, tables=None):
    T, D = x.shape
    N = w_bf16.shape[1]
    tm = ROW_TILE
    in_specs = [pl.BlockSpec((tm, D), lambda i: (i, 0)),
                pl.BlockSpec((D, N), lambda i: (0, 0))]
    args = [x, w_bf16]
    if mode == "rope":
        in_specs += [pl.BlockSpec((tm, LANES), lambda i: (i, 0))] * 2
        args += list(tables)
    return pl.pallas_call(
        functools.partial(_proj_kernel, mode=mode),
        grid=(T // tm,),
        in_specs=in_specs,
        out_specs=pl.BlockSpec((tm, N), lambda i: (i, 0)),
        out_shape=jax.ShapeDtypeStruct((T, N), out_dtype),
        compiler_params=_cparams("parallel"),
    )(*args)


def _compress_kernel(kv_ref, pos_ref, w1_ref, w2_ref, o_ref):
    kv = kv_ref[0].astype(F32)
    n = kv.shape[0]
    half = CMP_STRIDE * HEAD_DIM
    first = _dot((kv + pos_ref[0:1, :]).astype(BF16), w1_ref[0:half, :])
    second = _dot((kv + pos_ref[1:2, :]).astype(BF16), w1_ref[half:2 * half, :])
    h = first + pltpu.roll(second, n - 1, 0)
    g = 0.5 * h * (1.0 + jnp.tanh(math.sqrt(2.0 / math.pi) * (h + 0.044715 * (h * h * h))))
    o_ref[0] = _dot(g.astype(BF16), w2_ref[...]).astype(o_ref.dtype)


def _compress(kv, pos_emb, w1, w2):
    NB, S, _ = kv.shape
    n = S // CMP_STRIDE
    half = CMP_STRIDE * HEAD_DIM
    kvr = kv.reshape(NB, n, half)
    pos2 = pos_emb.reshape(2, half)
    return pl.pallas_call(
        _compress_kernel,
        grid=(NB,),
        in_specs=[pl.BlockSpec((1, n, half), lambda b: (b, 0, 0)),
                  pl.BlockSpec((2, half), lambda b: (0, 0)),
                  pl.BlockSpec((2 * half, CMP_HIDDEN), lambda b: (0, 0)),
                  pl.BlockSpec((CMP_HIDDEN, HEAD_DIM), lambda b: (0, 0))],
        out_specs=pl.BlockSpec((1, n, HEAD_DIM), lambda b: (b, 0, 0)),
        out_shape=jax.ShapeDtypeStruct((NB, n, HEAD_DIM), BF16),
        compiler_params=_cparams("parallel"),
    )(kvr, pos2, w1.astype(BF16), w2.astype(BF16))


def _store_heads_token_major(o_ref, out_t, tq, first_head):
    for g in range(0, out_t.shape[1] // tq, 2):
        pair = jnp.concatenate([out_t[:, g * tq:(g + 1) * tq], out_t[:, (g + 1) * tq:(g + 2) * tq]], axis=0)
        c0 = (first_head + g) * HEAD_DIM
        o_ref[:, c0:c0 + 2 * HEAD_DIM] = pair.T.astype(o_ref.dtype)


def _first_max_pick_t(score, idx, height):
    m = jnp.max(score, axis=0, keepdims=True)
    first = jnp.min(jnp.where(score == m, idx, height), axis=0, keepdims=True)
    return idx == first


def _nsa_cmp_kernel(qt_ref, kc_ref, vct_ref, oc_ref, qaug_ref, *, tq, n_slc, top_n):
    G = NSA_GROUP
    R = G * tq
    q0 = pl.program_id(1) * tq
    qt = qt_ref[0, 0]
    kc = kc_ref[0]
    ncp = kc.shape[0]
    s = _dot(kc, qt * SCALE)
    t_row = q0 + (lax.broadcasted_iota(jnp.int32, (1, R), 1) % tq)
    cmp_end = lax.broadcasted_iota(jnp.int32, (ncp, 1), 0) * CMP_STRIDE + (CMP_BLOCK - 1)
    s = jnp.where(cmp_end <= t_row, s, -jnp.inf)
    m = jnp.max(s, axis=0, keepdims=True)
    m = jnp.where(m == -jnp.inf, 0.0, m)
    e = jnp.exp(s - m)
    p = e * (1.0 / jnp.maximum(jnp.sum(e, axis=0, keepdims=True), 1e-30))
    _store_heads_token_major(oc_ref, _dot(vct_ref[0], p.astype(BF16)), tq, 0)

    pg = p[:, 0:tq]
    for g in range(1, G):
        pg = pg + p[:, g * tq:(g + 1) * tq]
    sj = lax.broadcasted_iota(jnp.int32, (n_slc, ncp), 0)
    ci = lax.broadcasted_iota(jnp.int32, (n_slc, ncp), 1)
    overlap = jnp.where(ci * CMP_STRIDE < (sj + 1) * SLC_BLOCK,
                        jnp.where(ci * CMP_STRIDE + CMP_BLOCK - 1 >= sj * SLC_BLOCK, 1.0, 0.0), 0.0)
    overlap = overlap.astype(BF16)
    p_hi, p_lo = _split2(pg)
    imp = _dot(overlap, p_hi) + _dot(overlap, p_lo)

    blk = lax.broadcasted_iota(jnp.int32, (n_slc, tq), 0)
    cur = (q0 + lax.broadcasted_iota(jnp.int32, (1, tq), 1)) // SLC_BLOCK
    forced = (blk == 0) | (blk == cur) | (blk == cur - 1)
    valid = blk <= cur
    score = jnp.where(forced, imp + FORCE_BONUS, jnp.where(valid, imp, -1.0))

    def pick(_, carry):
        score, sel = carry
        hit = _first_max_pick_t(score, blk, n_slc)
        return jnp.where(hit, -jnp.inf, score), jnp.where(hit, 1.0, sel)

    _, sel = lax.fori_loop(0, top_n, pick, (score, jnp.zeros((n_slc, tq), F32)))
    neg = jnp.where(valid, jnp.where(sel > 0.0, 0.0, MASK_NEG), MASK_NEG).astype(BF16)

    qaug_ref[0, 0, 0:HEAD_DIM, :] = qt
    qaug_ref[0, 0, HEAD_DIM:2 * HEAD_DIM, :] = jnp.zeros((HEAD_DIM, R), BF16)
    for g in range(G):
        qaug_ref[0, 0, 2 * HEAD_DIM:2 * HEAD_DIM + n_slc, g * tq:(g + 1) * tq] = neg


def _token_major_spec(tq, G, n_tiles, groups_per_batch):
    return pl.BlockSpec((tq, G * HEAD_DIM),
                        lambda b, i: ((b // groups_per_batch) * n_tiles + i, b % groups_per_batch))


def _nsa_compressed(qt, kc, vct, S):
    NB, n_tiles, _, R = qt.shape
    G = NSA_GROUP
    tq = R // G
    ncp = kc.shape[1]
    n_slc = S // SLC_BLOCK
    C = 2 * HEAD_DIM + n_slc
    kern = functools.partial(_nsa_cmp_kernel, tq=tq, n_slc=n_slc, top_n=min(SLC_TOPN, n_slc))
    return pl.pallas_call(
        kern,
        grid=(NB, n_tiles),
        in_specs=[pl.BlockSpec((1, 1, HEAD_DIM, R), lambda b, i: (b, i, 0, 0)),
                  pl.BlockSpec((1, ncp, HEAD_DIM), lambda b, i: (b, 0, 0)),
                  pl.BlockSpec((1, HEAD_DIM, ncp), lambda b, i: (b, 0, 0))],
        out_specs=[_token_major_spec(tq, G, n_tiles, NSA_KV_HEADS),
                   pl.BlockSpec((1, 1, C, R), lambda b, i: (b, i, 0, 0))],
        out_shape=[jax.ShapeDtypeStruct((NB * S // NSA_KV_HEADS, NSA_Q_W), BF16),
                   jax.ShapeDtypeStruct((NB, n_tiles, C, R), BF16)],
        compiler_params=_cparams("parallel", "parallel"),
    )(qt, kc, vct)


def _flash_kernel(qt_ref, k_ref, vt_ref, o_ref, *scratch, G, tq, tk, window, n_split):
    R = G * tq
    W = R // n_split
    VR = vt_ref.shape[1]
    q0 = pl.program_id(1) * tq
    q_scrs, s_scrs0, s_scrs1, p_scrs, acc_scrs = (scratch[i * n_split:(i + 1) * n_split] for i in range(5))
    s_bufs = (s_scrs0, s_scrs1)
    t_all = q0 + (lax.broadcasted_iota(jnp.int32, (1, R), 1) % tq)
    for h in range(n_split):
        q_scrs[h][...] = (qt_ref[0, 0, :, h * W:(h + 1) * W].astype(F32) * (SCALE * LOG2_E)).astype(BF16)
        acc_scrs[h][...] = jnp.zeros((VR, W), F32)

    def group(h, slot, vt_tile, k0, m, masked):
        s_scr, p_scr, acc_scr = s_bufs[slot][h], p_scrs[h], acc_scrs[h]
        t_row = t_all[:, h * W:(h + 1) * W]

        def load(r0, rows):
            s = s_scr[pl.ds(r0, rows), :]
            if masked:
                kpos = k0 + r0 + lax.broadcasted_iota(jnp.int32, (rows, 1), 0)
                if window is not None:
                    s = jnp.where(t_row - kpos < window, s, -jnp.inf)
                s = jnp.where(kpos <= t_row, s, -jnp.inf)
            return s

        def col_max(i, m8):
            return jnp.maximum(m8, load(pl.multiple_of(i * 8, 8), 8))

        m8 = lax.fori_loop(0, tk // 8, col_max, jnp.full((8, W), -jnp.inf, F32), unroll=True)
        m_new = jnp.maximum(m, jnp.max(m8, axis=0, keepdims=True))
        m_safe = jnp.where(m_new == -jnp.inf, 0.0, m_new)
        alpha = jnp.exp2(m - m_safe)

        def probs(i, carry):
            r0 = pl.multiple_of(i * 16, 16)
            p_scr[pl.ds(r0, 16), :] = jnp.exp2(load(r0, 16) - m_safe).astype(BF16)
            return carry

        lax.fori_loop(0, tk // 16, probs, 0, unroll=True)
        acc_scr[...] = alpha * acc_scr[...] + _dot(vt_tile, p_scr[...])
        return m_new

    def scores(j, slot):
        k_tile = k_ref[0, pl.ds(pl.multiple_of(j * tk, tk), tk), :]
        for h in range(n_split):
            s_bufs[slot][h][...] = _dot(k_tile, q_scrs[h][...])

    def softmax_pv(j, slot, ms, masked):
        k0 = pl.multiple_of(j * tk, tk)
        vt_tile = vt_ref[0, :, pl.ds(k0, tk)]
        return tuple(group(h, slot, vt_tile, k0, ms[h], masked) for h in range(n_split))

    def tile(j, ms):
        scores(j, 0)
        return softmax_pv(j, 0, ms, True)

    ms = tuple(jnp.full((1, W), -jnp.inf, F32) for _ in range(n_split))
    j_hi = (q0 + tq - 1) // tk
    if window is None:
        n_pairs = (q0 // tk) // 2

        def pair(i, ms):
            scores(2 * i + 1, 1)
            ms = softmax_pv(2 * i, 0, ms, False)
            scores(2 * i + 2, 0)
            return softmax_pv(2 * i + 1, 1, ms, False)

        scores(0, 0)
        ms = lax.fori_loop(0, n_pairs, pair, ms)
        ms = softmax_pv(2 * n_pairs, 0, ms, True)
        lax.fori_loop(2 * n_pairs + 1, j_hi + 1, tile, ms)
    else:
        j_lo = jnp.maximum(q0 - (window - 1), 0) // tk
        lax.fori_loop(j_lo, j_hi + 1, tile, ms)
    for h in range(n_split):
        acc = acc_scrs[h][...]
        out = acc[0:HEAD_DIM, :] / acc[HEAD_DIM:HEAD_DIM + 1, :]
        if G > 1:
            _store_heads_token_major(o_ref, out, tq, h * (G // n_split))
        else:
            o_ref[0, 0, :, h * W:(h + 1) * W] = out.astype(o_ref.dtype)


def _q_tiles_t(q, tq):
    NB, G, S, C = q.shape
    return q.reshape(NB, G, S // tq, tq, C).transpose(0, 2, 4, 1, 3).reshape(NB, S // tq, C, G * tq)


def _flash(qt, k, vt, *, G, tk, window=None):
    NB, n_tiles, C, R = qt.shape
    S = k.shape[1]
    tq = R // G
    tk = min(tk, S)
    ones = jnp.concatenate([jnp.ones((NB, 1, S), BF16), jnp.zeros((NB, V_ROWS - HEAD_DIM - 1, S), BF16)], axis=1)
    vt = jnp.concatenate([vt, ones], axis=1)
    ns = FLASH_SPLIT
    W = R // ns
    if G > 1:
        out_spec = _token_major_spec(tq, G, n_tiles, NSA_KV_HEADS)
        out_shape = jax.ShapeDtypeStruct((NB * S // NSA_KV_HEADS, NSA_KV_HEADS * G * HEAD_DIM), BF16)
    else:
        out_spec = pl.BlockSpec((1, 1, HEAD_DIM, tq), lambda b, i: (b, 0, 0, i))
        out_shape = jax.ShapeDtypeStruct((NB, 1, HEAD_DIM, S), BF16)
    kern = functools.partial(_flash_kernel, G=G, tq=tq, tk=tk, window=window, n_split=ns)
    return pl.pallas_call(
        kern,
        grid=(NB, S // tq),
        in_specs=[pl.BlockSpec((1, 1, C, R), lambda b, i: (b, i, 0, 0)),
                  pl.BlockSpec((1, S, C), lambda b, i: (b, 0, 0)),
                  pl.BlockSpec((1, V_ROWS, S), lambda b, i: (b, 0, 0))],
        out_specs=out_spec,
        out_shape=out_shape,
        scratch_shapes=([pltpu.VMEM((C, W), BF16)] * ns + [pltpu.VMEM((tk, W), F32)] * (2 * ns)
                        + [pltpu.VMEM((tk, W), BF16)] * ns + [pltpu.VMEM((V_ROWS, W), F32)] * ns),
        compiler_params=_cparams("parallel", "parallel"),
    )(qt, k, vt)


def _nsa_combine_kernel(oc_ref, os_ref, ow_ref, g_ref, o_ref):
    g = g_ref[...]
    for h in range(NSA_HEADS):
        sl = slice(h * HEAD_DIM, (h + 1) * HEAD_DIM)
        o = (g[:, 3 * h:3 * h + 1] * oc_ref[:, sl]
             + g[:, 3 * h + 1:3 * h + 2] * os_ref[:, sl]
             + g[:, 3 * h + 2:3 * h + 3] * ow_ref[:, sl])
        o_ref[:, sl] = o.astype(o_ref.dtype)


def _nsa_combine(oc, os_, ow, gates):
    T = oc.shape[0]
    tm = ROW_TILE
    spec = pl.BlockSpec((tm, NSA_Q_W), lambda i: (i, 0))
    return pl.pallas_call(
        _nsa_combine_kernel,
        grid=(T // tm,),
        in_specs=[spec, spec, spec, pl.BlockSpec((tm, LANES), lambda i: (i, 0))],
        out_specs=spec,
        out_shape=jax.ShapeDtypeStruct((T, NSA_Q_W), BF16),
        compiler_params=_cparams("parallel"),
    )(oc, os_, ow, gates)


def _block_mean_kernel(k_ref, o_ref, *, n_blk):
    k = k_ref[0].astype(F32).reshape(n_blk, MOBA_BLOCK, HEAD_DIM)
    o_ref[0] = (jnp.sum(k, axis=1) * (1.0 / MOBA_BLOCK)).astype(o_ref.dtype)


def _block_mean(k):
    NB, S, _ = k.shape
    n_blk = S // MOBA_BLOCK
    return pl.pallas_call(
        functools.partial(_block_mean_kernel, n_blk=n_blk),
        grid=(NB,),
        in_specs=[pl.BlockSpec((1, S, HEAD_DIM), lambda b: (b, 0, 0))],
        out_specs=pl.BlockSpec((1, n_blk, HEAD_DIM), lambda b: (b, 0, 0)),
        out_shape=jax.ShapeDtypeStruct((NB, n_blk, HEAD_DIM), BF16),
        compiler_params=_cparams("parallel"),
    )(k)


def _moba_gate_kernel(qt_ref, km_ref, qaug_ref, *, tq, n_blk, top_k):
    q0 = pl.program_id(1) * tq
    qt = qt_ref[0, 0]
    gate = _dot(km_ref[0], qt)
    blk = lax.broadcasted_iota(jnp.int32, (n_blk, tq), 0)
    cur = (q0 + lax.broadcasted_iota(jnp.int32, (1, tq), 1)) // MOBA_BLOCK
    past = blk < cur
    gate = jnp.where(past, gate, -jnp.inf)
    taken = jnp.zeros((n_blk, tq), F32)
    for _ in range(top_k):
        hit = _first_max_pick_t(jnp.where(taken > 0.0, -jnp.inf, gate), blk, n_blk)
        taken = jnp.where(hit, 1.0, taken)
    neg = jnp.where(blk == cur, 0.0,
                    jnp.where(past, jnp.where(taken > 0.0, 0.0, MASK_NEG), MASK_NEG)).astype(BF16)
    qaug_ref[0, 0, 0:HEAD_DIM, :] = qt
    qaug_ref[0, 0, HEAD_DIM:HEAD_DIM + n_blk, :] = neg
    if n_blk < MOBA_AUG:
        qaug_ref[0, 0, HEAD_DIM + n_blk:, :] = jnp.zeros((MOBA_AUG - n_blk, tq), BF16)


def _moba_gate(qt, kmean):
    NB, n_tiles, _, tq = qt.shape
    n_blk = kmean.shape[1]
    C = HEAD_DIM + MOBA_AUG
    kern = functools.partial(_moba_gate_kernel, tq=tq, n_blk=n_blk, top_k=min(MOBA_TOPK, n_blk))
    return pl.pallas_call(
        kern,
        grid=(NB, n_tiles),
        in_specs=[pl.BlockSpec((1, 1, HEAD_DIM, tq), lambda b, i: (b, i, 0, 0)),
                  pl.BlockSpec((1, n_blk, HEAD_DIM), lambda b, i: (b, 0, 0))],
        out_specs=pl.BlockSpec((1, 1, C, tq), lambda b, i: (b, i, 0, 0)),
        out_shape=jax.ShapeDtypeStruct((NB, n_tiles, C, tq), BF16),
        compiler_params=_cparams("parallel", "parallel"),
    )(qt, kmean)


def _sb_kernel(qt_ref, k_ref, vt_ref, o_ref, *acc_scrs, tile, hb):
    qi = pl.program_id(1)
    q0 = qi * tile
    t_row = q0 + lax.broadcasted_iota(jnp.int32, (1, tile), 1)
    later = (lax.broadcasted_iota(jnp.int32, (tile, tile), 1)
             > lax.broadcasted_iota(jnp.int32, (tile, tile), 0))
    later = jnp.where(later, 1.0, 0.0).astype(BF16)
    for acc in acc_scrs:
        acc[...] = jnp.zeros((HEAD_DIM, tile), F32)

    def step(j, carries, masked):
        k0 = pl.multiple_of(j * tile, tile)
        zs = [_dot(k_ref[h, pl.ds(k0, tile), :], qt_ref[h, 0] * SCALE) for h in range(hb)]
        if masked:
            mask = (k0 + lax.broadcasted_iota(jnp.int32, (tile, 1), 0)) < t_row
        sps = [jnp.maximum(z, 0.0) + jnp.log(1.0 + jnp.exp(-jnp.abs(z))) for z in zs]
        log_1ms = [jnp.where(mask, -sp, 0.0) if masked else -sp for sp in sps]
        parts = [_split2(x) for x in log_1ms]
        betweens = [_dot(later, hi) + _dot(later, lo) + c for (hi, lo), c in zip(parts, carries)]
        new_carries = []
        for h in range(hb):
            w = jnp.exp((zs[h] - sps[h]) + betweens[h])
            if masked:
                w = jnp.where(mask, w, 0.0)
            acc_scrs[h][...] += _dot(vt_ref[h, :, pl.ds(k0, tile)], w.astype(BF16))
            new_carries.append(carries[h] + jnp.sum(log_1ms[h], axis=0, keepdims=True))
        worst = new_carries[0]
        for c in new_carries[1:]:
            worst = jnp.maximum(worst, c)
        return tuple(new_carries), jnp.max(worst)

    carries, worst = step(qi, tuple(jnp.zeros((1, tile), F32) for _ in range(hb)), True)

    def cond(state):
        j, _, worst = state
        return (j >= 0) & (worst > SB_EXP_ZERO)

    def body(state):
        j, carries, _ = state
        carries, worst = step(j, carries, False)
        return j - 1, carries, worst

    lax.while_loop(cond, body, (qi - 1, carries, worst))
    for h in range(0, hb, 2):
        pair = jnp.concatenate([acc_scrs[h][...], acc_scrs[h + 1][...]], axis=0)
        o_ref[:, h * HEAD_DIM:(h + 2) * HEAD_DIM] = pair.T.astype(o_ref.dtype)


def _stick_breaking(q, k, vt, n_heads):
    NB, S, _ = q.shape
    tile = min(SB_TILE, S)
    hb = SB_HEADS_PER_STEP
    steps_per_batch = n_heads // hb
    qt = q.reshape(NB, S // tile, tile, HEAD_DIM).transpose(0, 1, 3, 2)
    return pl.pallas_call(
        functools.partial(_sb_kernel, tile=tile, hb=hb),
        grid=(NB // hb, S // tile),
        in_specs=[pl.BlockSpec((hb, 1, HEAD_DIM, tile), lambda b, i: (b, i, 0, 0)),
                  pl.BlockSpec((hb, S, HEAD_DIM), lambda b, i: (b, 0, 0)),
                  pl.BlockSpec((hb, HEAD_DIM, S), lambda b, i: (b, 0, 0))],
        out_specs=pl.BlockSpec((tile, hb * HEAD_DIM),
                               lambda b, i: ((b // steps_per_batch) * (S // tile) + i, b % steps_per_batch)),
        out_shape=jax.ShapeDtypeStruct((NB // n_heads * S, n_heads * HEAD_DIM), BF16),
        scratch_shapes=[pltpu.VMEM((HEAD_DIM, tile), F32)] * hb,
        compiler_params=_cparams("parallel", "parallel"),
    )(qt, k, vt)


def _layer_norm(y, g, b):
    mu = jnp.mean(y, axis=-1, keepdims=True)
    d = y - mu
    var = jnp.mean(d * d, axis=-1, keepdims=True)
    return d * lax.rsqrt(var + LN_EPS) * g + b


def _out_ln_kernel(*refs, n_in):
    x_ref = refs[0]
    o_refs = refs[1:1 + n_in]
    w_refs = refs[1 + n_in:1 + 2 * n_in]
    g_ref, b_ref, y_ref = refs[1 + 2 * n_in:]
    mix = _dot(o_refs[0][...].astype(BF16), w_refs[0][...])
    for o_ref, w_ref in zip(o_refs[1:], w_refs[1:]):
        mix += _dot(o_ref[...].astype(BF16), w_ref[...])
    y_ref[...] = _layer_norm(DEEPNORM_ALPHA * x_ref[...] + mix, g_ref[...], b_ref[...])


def _out_ln(x, outs, ws, g, b):
    T, D = x.shape
    tm = ROW_TILE
    n_in = len(outs)
    in_specs = [pl.BlockSpec((tm, D), lambda i: (i, 0))]
    in_specs += [pl.BlockSpec((tm, o.shape[1]), lambda i: (i, 0)) for o in outs]
    in_specs += [pl.BlockSpec(w.shape, lambda i: (0, 0)) for w in ws]
    in_specs += [pl.BlockSpec((1, D), lambda i: (0, 0))] * 2
    return pl.pallas_call(
        functools.partial(_out_ln_kernel, n_in=n_in),
        grid=(T // tm,),
        in_specs=in_specs,
        out_specs=pl.BlockSpec((tm, D), lambda i: (i, 0)),
        out_shape=jax.ShapeDtypeStruct((T, D), F32),
        compiler_params=_cparams("parallel"),
    )(x, *outs, *ws, g.reshape(1, D), b.reshape(1, D))


def _route(x, wr_ref, br_ref):
    x1 = x.astype(BF16)
    r1 = x - x1.astype(F32)
    x2 = r1.astype(BF16)
    x3 = (r1 - x2.astype(F32)).astype(BF16)
    w1, w2, w3 = wr_ref[0], wr_ref[1], wr_ref[2]
    logits = (_dot(x3, w1) + _dot(x2, w2) + _dot(x1, w3)
              + _dot(x2, w1) + _dot(x1, w2) + _dot(x1, w1) + br_ref[...])
    lane = lax.broadcasted_iota(jnp.int32, logits.shape, 1)
    is_grp = lane < N_GROUPS
    lg = jnp.where(is_grp, logits, -jnp.inf)
    mg = jnp.max(lg, axis=-1, keepdims=True)
    gidx = jnp.min(jnp.where(lg == mg, lane, LANES), axis=-1, keepdims=True)
    w_g = 1.0 / jnp.sum(jnp.where(is_grp, jnp.exp(logits - mg), 0.0), axis=-1, keepdims=True)
    first = N_GROUPS + gidx * EXPERTS_PER_GROUP
    in_grp = (lane >= first) & (lane < first + EXPERTS_PER_GROUP)
    le = jnp.where(in_grp, logits, -jnp.inf)
    v1 = jnp.max(le, axis=-1, keepdims=True)
    i1 = jnp.min(jnp.where(le == v1, lane, LANES), axis=-1, keepdims=True)
    le2 = jnp.where(lane == i1, -jnp.inf, le)
    v2 = jnp.max(le2, axis=-1, keepdims=True)
    i2 = jnp.min(jnp.where(le2 == v2, lane, LANES), axis=-1, keepdims=True)
    e2 = jnp.exp(v2 - v1)
    den = 1.0 + e2
    return jnp.where(lane == i1, (1.0 / den) * w_g, jnp.where(lane == i2, (e2 / den) * w_g, 0.0))


def _moe_kernel(x_ref, wr_ref, br_ref, wgu_ref, wd_ref, g_ref, b_ref, y_ref, comb_scr, acc_scr, xb_scr):
    x = x_ref[...]
    comb_scr[...] = _route(x, wr_ref, br_ref)
    xb_scr[...] = x.astype(BF16)
    acc_scr[...] = jnp.zeros_like(acc_scr)
    H = EXPERT_HIDDEN

    def expert(e, carry):
        xb = xb_scr[...]
        comb = comb_scr[...]
        lane = lax.broadcasted_iota(jnp.int32, comb.shape, 1)
        c = jnp.sum(jnp.where(lane == N_GROUPS + e, comb, 0.0), axis=-1, keepdims=True)
        gu = _dot(xb, wgu_ref[e])
        gate = gu[:, 0:H]
        h = (gate * (1.0 / (1.0 + jnp.exp(-gate)))) * gu[:, H:2 * H]
        acc_scr[...] += _dot((h * c).astype(BF16), wd_ref[e])
        return carry

    lax.fori_loop(0, N_EXPERTS, expert, 0)
    y_ref[...] = _layer_norm(DEEPNORM_ALPHA * x_ref[...] + acc_scr[...], g_ref[...], b_ref[...])


def _moe_ln(x, w_grp, b_grp, w_rt, b_rt, w_gate, w_up, w_down, g, b):
    T, D = x.shape
    tm = ROW_TILE
    wr = jnp.concatenate([w_grp, w_rt.transpose(1, 0, 2).reshape(D, N_EXPERTS)], axis=1)
    wr = jnp.pad(wr, ((0, 0), (0, LANES - wr.shape[1])))
    w1 = wr.astype(BF16)
    r1 = wr - w1.astype(F32)
    w2 = r1.astype(BF16)
    w3 = (r1 - w2.astype(F32)).astype(BF16)
    wr3 = jnp.stack([w1, w2, w3])
    br = jnp.pad(jnp.concatenate([b_grp, b_rt.reshape(N_EXPERTS)]), (0, LANES - N_GROUPS - N_EXPERTS))
    H = EXPERT_HIDDEN
    wgu = jnp.concatenate([w_gate, w_up], axis=2).astype(BF16)
    once = pl.Buffered(1)
    return pl.pallas_call(
        _moe_kernel,
        grid=(T // tm,),
        in_specs=[pl.BlockSpec((tm, D), lambda i: (i, 0)),
                  pl.BlockSpec((3, D, LANES), lambda i: (0, 0, 0), pipeline_mode=once),
                  pl.BlockSpec((1, LANES), lambda i: (0, 0), pipeline_mode=once),
                  pl.BlockSpec((N_EXPERTS, D, 2 * H), lambda i: (0, 0, 0), pipeline_mode=once),
                  pl.BlockSpec((N_EXPERTS, H, D), lambda i: (0, 0, 0), pipeline_mode=once),
                  pl.BlockSpec((1, D), lambda i: (0, 0), pipeline_mode=once),
                  pl.BlockSpec((1, D), lambda i: (0, 0), pipeline_mode=once)],
        out_specs=pl.BlockSpec((tm, D), lambda i: (i, 0)),
        out_shape=jax.ShapeDtypeStruct((T, D), F32),
        scratch_shapes=[pltpu.VMEM((tm, LANES), F32), pltpu.VMEM((tm, D), F32),
                        pltpu.VMEM((tm, D), BF16)],
        compiler_params=_cparams("parallel"),
    )(x, wr3, br.reshape(1, LANES), wgu, w_down.astype(BF16), g.reshape(1, D), b.reshape(1, D))


def _heads(a, B, S, n):
    return a.reshape(B, S, n, HEAD_DIM).transpose(0, 2, 1, 3).reshape(B * n, S, HEAD_DIM)


def _unheads(a, B, S, n):
    return a.reshape(B, n, S, HEAD_DIM).transpose(0, 2, 1, 3).reshape(B * S, n * HEAD_DIM)


def _heads_t(a, B, S, n):
    return a.reshape(B, S, n, HEAD_DIM).transpose(0, 2, 3, 1).reshape(B * n, HEAD_DIM, S)


def _unheads_t(a, B, S, n):
    return a.reshape(B, n, HEAD_DIM, S).transpose(0, 3, 1, 2).reshape(B * S, n * HEAD_DIM)


def _block_onehot(S, block, width):
    ids = np.arange(S)[:, None] // block == np.arange(width)[None, :]
    return jnp.asarray(ids, BF16)


def _nsa_moba_mixer(x, tables, B, S, w_in, cmp_pos_k, cmp_pos_v, cmp_w1_k, cmp_w2_k, cmp_w1_v, cmp_w2_v):
    T = B * S
    KV, G = NSA_KV_HEADS, NSA_GROUP
    widths = [NSA_Q_W] + [NSA_KV_W] * 6 + [NSA_GATE_W, MOBA_W, MOBA_W, MOBA_W]
    qa_w, kc_w, vc_w, ks_w, vs_w, kw_w, vw_w, ga_w, qm_w, km_w, vm_w = jnp.split(
        w_in, [int(v) for v in np.cumsum(widths)[:-1]], axis=1)
    w_rope = jnp.concatenate([qa_w, kc_w, ks_w, kw_w, qm_w, km_w], axis=1).astype(BF16)
    w_plain = jnp.concatenate([vc_w, vs_w, vw_w, vm_w], axis=1).astype(BF16)
    w_gate = jnp.pad(ga_w, ((0, 0), (0, LANES - NSA_GATE_W))).astype(BF16)

    roped = _project(x, w_rope, "rope", BF16, tables)
    plain = _project(x, w_plain, "plain", BF16)
    gates = _project(x, w_gate, "sigmoid", F32)

    o = 0
    qa = roped[:, o:o + NSA_Q_W]; o += NSA_Q_W
    kc = roped[:, o:o + NSA_KV_W]; o += NSA_KV_W
    ks = roped[:, o:o + NSA_KV_W]; o += NSA_KV_W
    kw = roped[:, o:o + NSA_KV_W]; o += NSA_KV_W
    qm = roped[:, o:o + MOBA_W]; o += MOBA_W
    km = roped[:, o:o + MOBA_W]
    vc = plain[:, 0:NSA_KV_W]
    vs = plain[:, NSA_KV_W:2 * NSA_KV_W]
    vw = plain[:, 2 * NSA_KV_W:3 * NSA_KV_W]
    vm = plain[:, 3 * NSA_KV_W:]

    qt = _q_tiles_t(_heads(qa, B, S, NSA_HEADS).reshape(B * KV, G, S, HEAD_DIM), min(NSA_FLASH_Q_TILE, S))
    kcc = _compress(_heads(kc, B, S, KV), cmp_pos_k, cmp_w1_k, cmp_w2_k)
    vcc = _compress(_heads(vc, B, S, KV), cmp_pos_v, cmp_w1_v, cmp_w2_v)
    o_c, qt_aug = _nsa_compressed(qt, kcc, vcc.transpose(0, 2, 1), S)
    n_slc = S // SLC_BLOCK
    k_sel = _heads(ks, B, S, KV)
    k_aug = jnp.concatenate(
        [k_sel, jnp.zeros((B * KV, S, HEAD_DIM), BF16),
         jnp.broadcast_to(_block_onehot(S, SLC_BLOCK, n_slc)[None], (B * KV, S, n_slc))], axis=-1)
    o_s = _flash(qt_aug, k_aug, _heads_t(vs, B, S, KV), G=G, tk=NSA_SEL_K_TILE)
    o_w = _flash(qt, _heads(kw, B, S, KV), _heads_t(vw, B, S, KV), G=G, tk=NSA_WIN_K_TILE, window=WINDOW)
    o_a = _nsa_combine(o_c, o_s, o_w, gates)

    H = MOBA_HEADS
    kh = _heads(km, B, S, H)
    qmt = _q_tiles_t(_heads(qm, B, S, H).reshape(B * H, 1, S, HEAD_DIM), min(MOBA_FLASH_Q_TILE, S))
    qmt_aug = _moba_gate(qmt, _block_mean(kh))
    km_aug = jnp.concatenate(
        [kh, jnp.broadcast_to(_block_onehot(S, MOBA_BLOCK, MOBA_AUG)[None], (B * H, S, MOBA_AUG))], axis=-1)
    o_b = _flash(qmt_aug, km_aug, _heads_t(vm, B, S, H), G=1, tk=MOBA_K_TILE)
    o_b = _unheads_t(o_b.reshape(B * H, HEAD_DIM, S), B, S, H)
    return o_a, o_b


def _sb_mixer(x, B, S, w_in):
    H = SB_HEADS
    qkv = _project(x, w_in.astype(BF16), "plain", BF16)
    q = _heads(qkv[:, 0:SB_W], B, S, H)
    k = _heads(qkv[:, SB_W:2 * SB_W], B, S, H)
    vt = _heads_t(qkv[:, 2 * SB_W:], B, S, H)
    return _stick_breaking(q, k, vt, H)


def kernel(x, positions, ab_w_in, ab_w_out, nsa_cmp_pos_k, nsa_cmp_pos_v, nsa_cmp_w1_k, nsa_cmp_w2_k,
           nsa_cmp_w1_v, nsa_cmp_w2_v, sb_w_in, sb_w_out, ln_mix_g, ln_mix_b, ln_ffn_g, ln_ffn_b,
           moe_w_grp, moe_b_grp, moe_w_rt, moe_b_rt, moe_w_gate, moe_w_up, moe_w_down):
    B, S, D = x.shape
    T = B * S
    assert S % ROW_TILE == 0 and S % MOBA_BLOCK == 0 and S // MOBA_BLOCK <= MOBA_AUG
    h = x.reshape(T, D)
    tables = _rope_tables(positions.reshape(T, 1).astype(F32))
    n_layers = ln_mix_g.shape[0]
    for layer in range(n_layers):
        i = layer // 2
        if layer % 2 == 0:
            o_a, o_b = _nsa_moba_mixer(h, tables, B, S, ab_w_in[i], nsa_cmp_pos_k[i], nsa_cmp_pos_v[i],
                                       nsa_cmp_w1_k[i], nsa_cmp_w2_k[i], nsa_cmp_w1_v[i], nsa_cmp_w2_v[i])
            w_out = ab_w_out[i].astype(BF16)
            h = _out_ln(h, [o_a, o_b], [w_out[:NSA_Q_W], w_out[NSA_Q_W:]], ln_mix_g[layer], ln_mix_b[layer])
        else:
            o = _sb_mixer(h, B, S, sb_w_in[i])
            h = _out_ln(h, [o], [sb_w_out[i].astype(BF16)], ln_mix_g[layer], ln_mix_b[layer])
        h = _moe_ln(h, moe_w_grp[layer], moe_b_grp[layer], moe_w_rt[layer], moe_b_rt[layer],
                    moe_w_gate[layer], moe_w_up[layer], moe_w_down[layer], ln_ffn_g[layer], ln_ffn_b[layer])
    return h.reshape(B, S, D)
```

```python
import functools
import math

import numpy as np
import jax
import jax.numpy as jnp
from jax import lax
from jax.experimental import pallas as pl
from jax.experimental.pallas import tpu as pltpu

F32 = jnp.float32
BF16 = jnp.bfloat16

LANES = 128
VMEM_LIMIT_BYTES = 56 * 1024 * 1024

HEAD_DIM = 64
ROPE_THETA = 10000.0
LN_EPS = 1e-5

NSA_HEADS = 8
NSA_KV_HEADS = 2
NSA_GROUP = NSA_HEADS // NSA_KV_HEADS
CMP_BLOCK = 32
CMP_STRIDE = 16
CMP_HIDDEN = 2 * HEAD_DIM
SLC_BLOCK = 64
SLC_TOPN = 16
WINDOW = 512
FORCE_BONUS = 1.0e4

MOBA_HEADS = 8
MOBA_BLOCK = 256
MOBA_TOPK = 3

SB_HEADS = 16

N_GROUPS = 4
EXPERTS_PER_GROUP = 4
N_EXPERTS = N_GROUPS * EXPERTS_PER_GROUP
EXPERT_HIDDEN = 256

DEPTH = 2
DEEPNORM_ALPHA = float((2 * DEPTH) ** 0.25)

NSA_Q_W = NSA_HEADS * HEAD_DIM
NSA_KV_W = NSA_KV_HEADS * HEAD_DIM
NSA_GATE_W = 3 * NSA_HEADS
MOBA_W = MOBA_HEADS * HEAD_DIM
SB_W = SB_HEADS * HEAD_DIM

SCALE = HEAD_DIM ** -0.5
LOG2_E = 1.0 / math.log(2.0)
BF16_SUBLANES = 16
V_ROWS = -(-(HEAD_DIM + 1) // BF16_SUBLANES) * BF16_SUBLANES
FLASH_SPLIT = 2
MASK_NEG = -(2.0 ** 60)
SB_EXP_ZERO = -110.0

ROW_TILE = 512
PROJ_CHUNK = 512
NSA_Q_TILE = 128
NSA_FLASH_Q_TILE = 256
NSA_SEL_K_TILE = 512
NSA_WIN_K_TILE = 256
MOBA_FLASH_Q_TILE = 512
MOBA_K_TILE = 512
MOBA_AUG = 64
SB_TILE = 256
SB_HEADS_PER_STEP = 4


def _cparams(*sem):
    return pltpu.CompilerParams(dimension_semantics=sem, vmem_limit_bytes=VMEM_LIMIT_BYTES)


def _dot(a, b):
    return jnp.dot(a, b, preferred_element_type=F32)


def _split2(x):
    hi = x.astype(BF16)
    lo = (x - hi.astype(F32)).astype(BF16)
    return hi, lo


def _rope_table_kernel(pos_ref, inv_ref, cos_ref, sin_ref):
    ang = pos_ref[...] * inv_ref[...]
    lane = lax.broadcasted_iota(jnp.int32, ang.shape, 1)
    sign = jnp.where((lane % HEAD_DIM) < HEAD_DIM // 2, -1.0, 1.0)
    cos_ref[...] = jnp.cos(ang)
    sin_ref[...] = jnp.sin(ang) * sign


def _rope_tables(pos_f32):
    T = pos_f32.shape[0]
    half = HEAD_DIM // 2
    inv = ROPE_THETA ** (-np.arange(half, dtype=np.float64) / half)
    inv_row = jnp.asarray(np.tile(inv, LANES // half)[None, :], F32)
    tm = ROW_TILE
    return pl.pallas_call(
        _rope_table_kernel,
        grid=(T // tm,),
        in_specs=[pl.BlockSpec((tm, 1), lambda i: (i, 0)),
                  pl.BlockSpec((1, LANES), lambda i: (0, 0))],
        out_specs=[pl.BlockSpec((tm, LANES), lambda i: (i, 0))] * 2,
        out_shape=[jax.ShapeDtypeStruct((T, LANES), F32)] * 2,
        compiler_params=_cparams("parallel"),
    )(pos_f32, inv_row)


def _proj_heads_kernel(*refs, plan, tm, tiles_per_seq, with_rope):
    if with_rope:
        x_ref, w_ref, cos_ref, sin_ref = refs[:4]
        out_refs = refs[4:]
        cos = cos_ref[...]
        sin = sin_ref[...]
        lane = lax.broadcasted_iota(jnp.int32, cos.shape, 1)
        first_half = (lane % HEAD_DIM) < HEAD_DIM // 2
    else:
        x_ref, w_ref = refs[:2]
        out_refs = refs[2:]
    xb = x_ref[...].astype(BF16)
    pos = (pl.program_id(0) % tiles_per_seq) * tm + lax.broadcasted_iota(jnp.int32, (tm, 1), 0)
    chunks = {}

    def pair(c0, rope):
        k = c0 // PROJ_CHUNK
        if k not in chunks:
            lo = k * PROJ_CHUNK
            chunks[k] = _dot(xb, w_ref[:, lo:min(lo + PROJ_CHUNK, w_ref.shape[1])])
        a = chunks[k][:, c0 - k * PROJ_CHUNK:c0 - k * PROJ_CHUNK + LANES]
        if rope:
            swapped = jnp.where(first_half, pltpu.roll(a, LANES - HEAD_DIM // 2, 1),
                                pltpu.roll(a, HEAD_DIM // 2, 1))
            a = a * cos + swapped * sin
        return a

    for (kind, col0, n_heads, rope, prm), o_ref in zip(plan, out_refs):
        if kind == "sigmoid":
            o_ref[...] = 1.0 / (1.0 + jnp.exp(-pair(col0, False)))
            continue
        for h in range(0, n_heads, 2):
            a = pair(col0 + h * HEAD_DIM, rope)
            if kind in ("rows", "rows_aug"):
                for d in range(2):
                    o_ref[h + d, :, 0:HEAD_DIM] = a[:, d * HEAD_DIM:(d + 1) * HEAD_DIM].astype(o_ref.dtype)
            else:
                at = a.T
                for d in range(2):
                    head = at[d * HEAD_DIM:(d + 1) * HEAD_DIM, :].astype(o_ref.dtype)
                    if kind == "cols":
                        o_ref[h + d, 0:HEAD_DIM, :] = head
                    else:
                        G, tq = prm
                        grp, g = divmod(h + d, G)
                        if tq <= tm:
                            for u in range(tm // tq):
                                o_ref[grp, u, :, g * tq:(g + 1) * tq] = head[:, u * tq:(u + 1) * tq]
                        else:
                            o_ref[grp, 0, :, :] = head
        if kind == "rows_aug":
            block, off, width = prm
            C = o_ref.shape[2]
            onehot = jnp.where(pos // block == lax.broadcasted_iota(jnp.int32, (tm, width), 1), 1.0, 0.0)
            for h in range(n_heads):
                if off > HEAD_DIM:
                    o_ref[h, :, HEAD_DIM:off] = jnp.zeros((tm, off - HEAD_DIM), o_ref.dtype)
                o_ref[h, :, off:off + width] = onehot.astype(o_ref.dtype)
                if off + width < C:
                    o_ref[h, :, off + width:C] = jnp.zeros((tm, C - off - width), o_ref.dtype)
        if kind == "cols" and o_ref.shape[1] > HEAD_DIM:
            VR = o_ref.shape[1]
            extra = jnp.where(lax.broadcasted_iota(jnp.int32, (VR - HEAD_DIM, tm), 0) == 0, 1.0, 0.0)
            for h in range(n_heads):
                o_ref[h, HEAD_DIM:VR, :] = extra.astype(o_ref.dtype)


def _project_heads(x, w_bf16, plan, B, S, tables=None):
    T, D = x.shape
    N = w_bf16.shape[1]
    tm = ROW_TILE
    tps = S // tm
    in_specs = [pl.BlockSpec((tm, D), lambda i: (i, 0)),
                pl.BlockSpec((D, N), lambda i: (0, 0), pipeline_mode=pl.Buffered(1))]
    args = [x, w_bf16]
    if tables is not None:
        in_specs += [pl.BlockSpec((tm, LANES), lambda i: (i, 0))] * 2
        args += list(tables)
    out_specs, out_shapes = [], []
    for kind, col0, n, rope, prm in plan:
        if kind == "sigmoid":
            out_specs.append(pl.BlockSpec((tm, LANES), lambda i: (i, 0)))
            out_shapes.append(jax.ShapeDtypeStruct((T, LANES), F32))
        elif kind == "rows":
            out_specs.append(pl.BlockSpec((n, tm, HEAD_DIM), lambda i: (i // tps, i % tps, 0)))
            out_shapes.append(jax.ShapeDtypeStruct((B * n, S, HEAD_DIM), BF16))
        elif kind == "rows_aug":
            C = prm[3]
            out_specs.append(pl.BlockSpec((n, tm, C), lambda i: (i // tps, i % tps, 0)))
            out_shapes.append(jax.ShapeDtypeStruct((B * n, S, C), BF16))
        elif kind == "cols":
            VR = prm
            out_specs.append(pl.BlockSpec((n, VR, tm), lambda i: (i // tps, 0, i % tps)))
            out_shapes.append(jax.ShapeDtypeStruct((B * n, VR, S), BF16))
        else:
            G, tq = prm
            ng = n // G
            if tq <= tm:
                out_specs.append(pl.BlockSpec((ng, tm // tq, HEAD_DIM, G * tq), lambda i: (i // tps, i % tps, 0, 0)))
            else:
                r = tq // tm
                out_specs.append(pl.BlockSpec((ng, 1, HEAD_DIM, tm),
                                              lambda i, r=r: (i // tps, (i % tps) // r, 0, (i % tps) % r)))
            out_shapes.append(jax.ShapeDtypeStruct((B * ng, S // tq, HEAD_DIM, G * tq), BF16))
    kplan = tuple((k, c, n, r, (p[:3] if k == "rows_aug" else p)) for k, c, n, r, p in plan)
    return pl.pallas_call(
        functools.partial(_proj_heads_kernel, plan=kplan, tm=tm, tiles_per_seq=tps, with_rope=tables is not None),
        grid=(T // tm,),
        in_specs=in_specs,
        out_specs=out_specs,
        out_shape=out_shapes,
        compiler_params=_cparams("parallel"),
    )(*args)


def _compress_kernel(kv_ref, pos_ref, w1_ref, w2_ref, o_ref):
    kv = kv_ref[0].astype(F32)
    n = kv.shape[0]
    half = CMP_STRIDE * HEAD_DIM
    first = _dot((kv + pos_ref[0:1, :]).astype(BF16), w1_ref[0:half, :])
    second = _dot((kv + pos_ref[1:2, :]).astype(BF16), w1_ref[half:2 * half, :])
    h = first + pltpu.roll(second, n - 1, 0)
    g = 0.5 * h * (1.0 + jnp.tanh(math.sqrt(2.0 / math.pi) * (h + 0.044715 * (h * h * h))))
    o_ref[0] = _dot(g.astype(BF16), w2_ref[...]).astype(o_ref.dtype)


def _compress(kv, pos_emb, w1, w2):
    NB, S, _ = kv.shape
    n = S // CMP_STRIDE
    half = CMP_STRIDE * HEAD_DIM
    kvr = kv.reshape(NB, n, half)
    pos2 = pos_emb.reshape(2, half)
    return pl.pallas_call(
        _compress_kernel,
        grid=(NB,),
        in_specs=[pl.BlockSpec((1, n, half), lambda b: (b, 0, 0)),
                  pl.BlockSpec((2, half), lambda b: (0, 0)),
                  pl.BlockSpec((2 * half, CMP_HIDDEN), lambda b: (0, 0)),
                  pl.BlockSpec((CMP_HIDDEN, HEAD_DIM), lambda b: (0, 0))],
        out_specs=pl.BlockSpec((1, n, HEAD_DIM), lambda b: (b, 0, 0)),
        out_shape=jax.ShapeDtypeStruct((NB, n, HEAD_DIM), BF16),
        compiler_params=_cparams("parallel"),
    )(kvr, pos2, w1.astype(BF16), w2.astype(BF16))


def _store_heads_token_major(o_ref, out_t, tq, first_head):
    for g in range(0, out_t.shape[1] // tq, 2):
        pair = jnp.concatenate([out_t[:, g * tq:(g + 1) * tq], out_t[:, (g + 1) * tq:(g + 2) * tq]], axis=0)
        c0 = (first_head + g) * HEAD_DIM
        o_ref[:, c0:c0 + 2 * HEAD_DIM] = pair.T.astype(o_ref.dtype)


def _first_max_pick_t(score, idx, height):
    m = jnp.max(score, axis=0, keepdims=True)
    first = jnp.min(jnp.where(score == m, idx, height), axis=0, keepdims=True)
    return idx == first


def _nsa_cmp_kernel(qt_ref, kc_ref, vct_ref, oc_ref, qaug_ref, *, tq, n_slc, top_n):
    G = NSA_GROUP
    R = G * tq
    q0 = pl.program_id(1) * tq
    qt = qt_ref[0, 0]
    kc = kc_ref[0]
    ncp = kc.shape[0]
    s = _dot(kc, qt * SCALE)
    t_row = q0 + (lax.broadcasted_iota(jnp.int32, (1, R), 1) % tq)
    cmp_end = lax.broadcasted_iota(jnp.int32, (ncp, 1), 0) * CMP_STRIDE + (CMP_BLOCK - 1)
    s = jnp.where(cmp_end <= t_row, s, -jnp.inf)
    m = jnp.max(s, axis=0, keepdims=True)
    m = jnp.where(m == -jnp.inf, 0.0, m)
    e = jnp.exp(s - m)
    p = e * (1.0 / jnp.maximum(jnp.sum(e, axis=0, keepdims=True), 1e-30))
    _store_heads_token_major(oc_ref, _dot(vct_ref[0], p.astype(BF16)), tq, 0)

    pg = p[:, 0:tq]
    for g in range(1, G):
        pg = pg + p[:, g * tq:(g + 1) * tq]
    sj = lax.broadcasted_iota(jnp.int32, (n_slc, ncp), 0)
    ci = lax.broadcasted_iota(jnp.int32, (n_slc, ncp), 1)
    overlap = jnp.where(ci * CMP_STRIDE < (sj + 1) * SLC_BLOCK,
                        jnp.where(ci * CMP_STRIDE + CMP_BLOCK - 1 >= sj * SLC_BLOCK, 1.0, 0.0), 0.0)
    overlap = overlap.astype(BF16)
    p_hi, p_lo = _split2(pg)
    imp = _dot(overlap, p_hi) + _dot(overlap, p_lo)

    blk = lax.broadcasted_iota(jnp.int32, (n_slc, tq), 0)
    cur = (q0 + lax.broadcasted_iota(jnp.int32, (1, tq), 1)) // SLC_BLOCK
    forced = (blk == 0) | (blk == cur) | (blk == cur - 1)
    valid = blk <= cur
    score = jnp.where(forced, imp + FORCE_BONUS, jnp.where(valid, imp, -1.0))

    def pick(_, carry):
        score, sel = carry
        hit = _first_max_pick_t(score, blk, n_slc)
        return jnp.where(hit, -jnp.inf, score), jnp.where(hit, 1.0, sel)

    _, sel = lax.fori_loop(0, top_n, pick, (score, jnp.zeros((n_slc, tq), F32)))
    neg = jnp.where(valid, jnp.where(sel > 0.0, 0.0, MASK_NEG), MASK_NEG).astype(BF16)

    qaug_ref[0, 0, 0:HEAD_DIM, :] = qt
    qaug_ref[0, 0, HEAD_DIM:2 * HEAD_DIM, :] = jnp.zeros((HEAD_DIM, R), BF16)
    for g in range(G):
        qaug_ref[0, 0, 2 * HEAD_DIM:2 * HEAD_DIM + n_slc, g * tq:(g + 1) * tq] = neg


def _token_major_spec(tq, G, n_tiles, groups_per_batch):
    return pl.BlockSpec((tq, G * HEAD_DIM),
                        lambda b, i: ((b // groups_per_batch) * n_tiles + i, b % groups_per_batch))


def _nsa_compressed(qt, kc, vct, S):
    NB, n_tiles, _, R = qt.shape
    G = NSA_GROUP
    tq = R // G
    ncp = kc.shape[1]
    n_slc = S // SLC_BLOCK
    C = 2 * HEAD_DIM + n_slc
    kern = functools.partial(_nsa_cmp_kernel, tq=tq, n_slc=n_slc, top_n=min(SLC_TOPN, n_slc))
    return pl.pallas_call(
        kern,
        grid=(NB, n_tiles),
        in_specs=[pl.BlockSpec((1, 1, HEAD_DIM, R), lambda b, i: (b, i, 0, 0)),
                  pl.BlockSpec((1, ncp, HEAD_DIM), lambda b, i: (b, 0, 0)),
                  pl.BlockSpec((1, HEAD_DIM, ncp), lambda b, i: (b, 0, 0))],
        out_specs=[_token_major_spec(tq, G, n_tiles, NSA_KV_HEADS),
                   pl.BlockSpec((1, 1, C, R), lambda b, i: (b, i, 0, 0))],
        out_shape=[jax.ShapeDtypeStruct((NB * S // NSA_KV_HEADS, NSA_Q_W), BF16),
                   jax.ShapeDtypeStruct((NB, n_tiles, C, R), BF16)],
        compiler_params=_cparams("parallel", "parallel"),
    )(qt, kc, vct)


def _flash_kernel(qt_ref, k_ref, vt_ref, o_ref, *scratch, G, tq, tk, window, n_split, kv_heads):
    R = G * tq * kv_heads
    W = R // n_split
    VR = vt_ref.shape[1]
    q0 = pl.program_id(1) * tq
    q_scrs, s_scrs0, s_scrs1, p_scrs, acc_scrs = (scratch[i * n_split:(i + 1) * n_split] for i in range(5))
    s_bufs = (s_scrs0, s_scrs1)
    t_all = q0 + (lax.broadcasted_iota(jnp.int32, (1, R), 1) % tq)
    for h in range(n_split):
        q_h = qt_ref[0, 0, :, h * W:(h + 1) * W] if kv_heads == 1 else qt_ref[h, 0]
        q_scrs[h][...] = (q_h.astype(F32) * (SCALE * LOG2_E)).astype(BF16)
        acc_scrs[h][...] = jnp.zeros((VR, W), F32)

    def group(h, slot, vt_tile, k0, m, masked):
        s_scr, p_scr, acc_scr = s_bufs[slot][h], p_scrs[h], acc_scrs[h]
        t_row = t_all[:, h * W:(h + 1) * W]

        def load(r0, rows):
            s = s_scr[pl.ds(r0, rows), :]
            if masked:
                kpos = k0 + r0 + lax.broadcasted_iota(jnp.int32, (rows, 1), 0)
                if window is not None:
                    s = jnp.where(t_row - kpos < window, s, -jnp.inf)
                s = jnp.where(kpos <= t_row, s, -jnp.inf)
            return s

        def col_max(i, m8):
            return jnp.maximum(m8, load(pl.multiple_of(i * 8, 8), 8))

        m8 = lax.fori_loop(0, tk // 8, col_max, jnp.full((8, W), -jnp.inf, F32), unroll=True)
        m_new = jnp.maximum(m, jnp.max(m8, axis=0, keepdims=True))
        m_safe = jnp.where(m_new == -jnp.inf, 0.0, m_new)
        alpha = jnp.exp2(m - m_safe)

        def probs(i, carry):
            r0 = pl.multiple_of(i * 16, 16)
            p_scr[pl.ds(r0, 16), :] = jnp.exp2(load(r0, 16) - m_safe).astype(BF16)
            return carry

        lax.fori_loop(0, tk // 16, probs, 0, unroll=True)
        acc_scr[...] = alpha * acc_scr[...] + _dot(vt_tile, p_scr[...])
        return m_new

    def scores(j, slot):
        k0 = pl.multiple_of(j * tk, tk)
        for h in range(n_split):
            k_tile = k_ref[h if kv_heads > 1 else 0, pl.ds(k0, tk), :]
            s_bufs[slot][h][...] = _dot(k_tile, q_scrs[h][...])

    def softmax_pv(j, slot, ms, masked):
        k0 = pl.multiple_of(j * tk, tk)
        return tuple(group(h, slot, vt_ref[h if kv_heads > 1 else 0, :, pl.ds(k0, tk)], k0, ms[h], masked)
                     for h in range(n_split))

    def tile(j, ms):
        scores(j, 0)
        return softmax_pv(j, 0, ms, True)

    ms = tuple(jnp.full((1, W), -jnp.inf, F32) for _ in range(n_split))
    j_hi = (q0 + tq - 1) // tk
    if window is None:
        n_pairs = (q0 // tk) // 2

        def pair(i, ms):
            scores(2 * i + 1, 1)
            ms = softmax_pv(2 * i, 0, ms, False)
            scores(2 * i + 2, 0)
            return softmax_pv(2 * i + 1, 1, ms, False)

        scores(0, 0)
        ms = lax.fori_loop(0, n_pairs, pair, ms)
        ms = softmax_pv(2 * n_pairs, 0, ms, True)
        lax.fori_loop(2 * n_pairs + 1, j_hi + 1, tile, ms)
    else:
        j_lo = jnp.maximum(q0 - (window - 1), 0) // tk
        lax.fori_loop(j_lo, j_hi + 1, tile, ms)
    outs = []
    for h in range(n_split):
        acc = acc_scrs[h][...]
        outs.append(acc[0:HEAD_DIM, :] / acc[HEAD_DIM:HEAD_DIM + 1, :])
    _store_heads_token_major(o_ref, jnp.concatenate(outs, axis=1), tq, 0)


def _flash(qt, k, vt, *, G, kv_heads, heads_per_batch, tk, window=None):
    NB, n_tiles, C, GW = qt.shape
    S = k.shape[1]
    tq = GW // G
    tk = min(tk, S)
    ns = FLASH_SPLIT
    assert kv_heads in (1, ns) and (kv_heads == 1 or G == 1)
    W = GW * kv_heads // ns
    per_step = G * kv_heads
    out_spec = _token_major_spec(tq, per_step, n_tiles, heads_per_batch // per_step)
    out_shape = jax.ShapeDtypeStruct((NB * G // heads_per_batch * S, heads_per_batch * HEAD_DIM), BF16)
    kern = functools.partial(_flash_kernel, G=G, tq=tq, tk=tk, window=window, n_split=ns, kv_heads=kv_heads)
    return pl.pallas_call(
        kern,
        grid=(NB // kv_heads, S // tq),
        in_specs=[pl.BlockSpec((kv_heads, 1, C, GW), lambda b, i: (b, i, 0, 0)),
                  pl.BlockSpec((kv_heads, S, C), lambda b, i: (b, 0, 0)),
                  pl.BlockSpec((kv_heads, V_ROWS, S), lambda b, i: (b, 0, 0))],
        out_specs=out_spec,
        out_shape=out_shape,
        scratch_shapes=([pltpu.VMEM((C, W), BF16)] * ns + [pltpu.VMEM((tk, W), F32)] * (2 * ns)
                        + [pltpu.VMEM((tk, W), BF16)] * ns + [pltpu.VMEM((V_ROWS, W), F32)] * ns),
        compiler_params=_cparams("parallel", "parallel"),
    )(qt, k, vt)


def _nsa_combine_kernel(oc_ref, os_ref, ow_ref, g_ref, o_ref):
    g = g_ref[...]
    for h in range(NSA_HEADS):
        sl = slice(h * HEAD_DIM, (h + 1) * HEAD_DIM)
        o = (g[:, 3 * h:3 * h + 1] * oc_ref[:, sl]
             + g[:, 3 * h + 1:3 * h + 2] * os_ref[:, sl]
             + g[:, 3 * h + 2:3 * h + 3] * ow_ref[:, sl])
        o_ref[:, sl] = o.astype(o_ref.dtype)


def _nsa_combine(oc, os_, ow, gates):
    T = oc.shape[0]
    tm = ROW_TILE
    spec = pl.BlockSpec((tm, NSA_Q_W), lambda i: (i, 0))
    return pl.pallas_call(
        _nsa_combine_kernel,
        grid=(T // tm,),
        in_specs=[spec, spec, spec, pl.BlockSpec((tm, LANES), lambda i: (i, 0))],
        out_specs=spec,
        out_shape=jax.ShapeDtypeStruct((T, NSA_Q_W), BF16),
        compiler_params=_cparams("parallel"),
    )(oc, os_, ow, gates)


def _block_mean_kernel(k_ref, o_ref, *, n_blk):
    k = k_ref[0][:, 0:HEAD_DIM].astype(F32).reshape(n_blk, MOBA_BLOCK, HEAD_DIM)
    o_ref[0] = (jnp.sum(k, axis=1) * (1.0 / MOBA_BLOCK)).astype(o_ref.dtype)


def _block_mean(k):
    NB, S, C = k.shape
    n_blk = S // MOBA_BLOCK
    return pl.pallas_call(
        functools.partial(_block_mean_kernel, n_blk=n_blk),
        grid=(NB,),
        in_specs=[pl.BlockSpec((1, S, C), lambda b: (b, 0, 0))],
        out_specs=pl.BlockSpec((1, n_blk, HEAD_DIM), lambda b: (b, 0, 0)),
        out_shape=jax.ShapeDtypeStruct((NB, n_blk, HEAD_DIM), BF16),
        compiler_params=_cparams("parallel"),
    )(k)


def _moba_gate_kernel(qt_ref, km_ref, qaug_ref, *, tq, n_blk, top_k):
    q0 = pl.program_id(1) * tq
    qt = qt_ref[0, 0]
    gate = _dot(km_ref[0], qt)
    blk = lax.broadcasted_iota(jnp.int32, (n_blk, tq), 0)
    cur = (q0 + lax.broadcasted_iota(jnp.int32, (1, tq), 1)) // MOBA_BLOCK
    past = blk < cur
    gate = jnp.where(past, gate, -jnp.inf)
    taken = jnp.zeros((n_blk, tq), F32)
    for _ in range(top_k):
        hit = _first_max_pick_t(jnp.where(taken > 0.0, -jnp.inf, gate), blk, n_blk)
        taken = jnp.where(hit, 1.0, taken)
    neg = jnp.where(blk == cur, 0.0,
                    jnp.where(past, jnp.where(taken > 0.0, 0.0, MASK_NEG), MASK_NEG)).astype(BF16)
    qaug_ref[0, 0, 0:HEAD_DIM, :] = qt
    qaug_ref[0, 0, HEAD_DIM:HEAD_DIM + n_blk, :] = neg
    if n_blk < MOBA_AUG:
        qaug_ref[0, 0, HEAD_DIM + n_blk:, :] = jnp.zeros((MOBA_AUG - n_blk, tq), BF16)


def _moba_gate(qt, kmean):
    NB, n_tiles, _, tq = qt.shape
    n_blk = kmean.shape[1]
    C = HEAD_DIM + MOBA_AUG
    kern = functools.partial(_moba_gate_kernel, tq=tq, n_blk=n_blk, top_k=min(MOBA_TOPK, n_blk))
    return pl.pallas_call(
        kern,
        grid=(NB, n_tiles),
        in_specs=[pl.BlockSpec((1, 1, HEAD_DIM, tq), lambda b, i: (b, i, 0, 0)),
                  pl.BlockSpec((1, n_blk, HEAD_DIM), lambda b, i: (b, 0, 0))],
        out_specs=pl.BlockSpec((1, 1, C, tq), lambda b, i: (b, i, 0, 0)),
        out_shape=jax.ShapeDtypeStruct((NB, n_tiles, C, tq), BF16),
        compiler_params=_cparams("parallel", "parallel"),
    )(qt, kmean)


def _sb_kernel(qt_ref, k_ref, vt_ref, o_ref, *acc_scrs, tile, hb):
    qi = pl.program_id(1)
    q0 = qi * tile
    t_row = q0 + lax.broadcasted_iota(jnp.int32, (1, tile), 1)
    later = (lax.broadcasted_iota(jnp.int32, (tile, tile), 1)
             > lax.broadcasted_iota(jnp.int32, (tile, tile), 0))
    later = jnp.where(later, 1.0, 0.0).astype(BF16)
    for acc in acc_scrs:
        acc[...] = jnp.zeros((HEAD_DIM, tile), F32)

    def step(j, carries, masked):
        k0 = pl.multiple_of(j * tile, tile)
        zs = [_dot(k_ref[h, pl.ds(k0, tile), :], qt_ref[h, 0] * SCALE) for h in range(hb)]
        if masked:
            mask = (k0 + lax.broadcasted_iota(jnp.int32, (tile, 1), 0)) < t_row
        sps = [jnp.maximum(z, 0.0) + jnp.log(1.0 + jnp.exp(-jnp.abs(z))) for z in zs]
        log_1ms = [jnp.where(mask, -sp, 0.0) if masked else -sp for sp in sps]
        parts = [_split2(x) for x in log_1ms]
        betweens = [_dot(later, hi) + _dot(later, lo) + c for (hi, lo), c in zip(parts, carries)]
        new_carries = []
        for h in range(hb):
            w = jnp.exp((zs[h] - sps[h]) + betweens[h])
            if masked:
                w = jnp.where(mask, w, 0.0)
            acc_scrs[h][...] += _dot(vt_ref[h, :, pl.ds(k0, tile)], w.astype(BF16))
            new_carries.append(carries[h] + jnp.sum(log_1ms[h], axis=0, keepdims=True))
        worst = new_carries[0]
        for c in new_carries[1:]:
            worst = jnp.maximum(worst, c)
        return tuple(new_carries), jnp.max(worst)

    carries, worst = step(qi, tuple(jnp.zeros((1, tile), F32) for _ in range(hb)), True)

    def cond(state):
        j, _, worst = state
        return (j >= 0) & (worst > SB_EXP_ZERO)

    def body(state):
        j, carries, _ = state
        carries, worst = step(j, carries, False)
        return j - 1, carries, worst

    lax.while_loop(cond, body, (qi - 1, carries, worst))
    for h in range(0, hb, 2):
        pair = jnp.concatenate([acc_scrs[h][...], acc_scrs[h + 1][...]], axis=0)
        o_ref[:, h * HEAD_DIM:(h + 2) * HEAD_DIM] = pair.T.astype(o_ref.dtype)


def _stick_breaking(qt, k, vt, n_heads):
    NB, S, _ = k.shape
    tile = qt.shape[3]
    hb = SB_HEADS_PER_STEP
    steps_per_batch = n_heads // hb
    return pl.pallas_call(
        functools.partial(_sb_kernel, tile=tile, hb=hb),
        grid=(NB // hb, S // tile),
        in_specs=[pl.BlockSpec((hb, 1, HEAD_DIM, tile), lambda b, i: (b, i, 0, 0)),
                  pl.BlockSpec((hb, S, HEAD_DIM), lambda b, i: (b, 0, 0)),
                  pl.BlockSpec((hb, HEAD_DIM, S), lambda b, i: (b, 0, 0))],
        out_specs=pl.BlockSpec((tile, hb * HEAD_DIM),
                               lambda b, i: ((b // steps_per_batch) * (S // tile) + i, b % steps_per_batch)),
        out_shape=jax.ShapeDtypeStruct((NB // n_heads * S, n_heads * HEAD_DIM), BF16),
        scratch_shapes=[pltpu.VMEM((HEAD_DIM, tile), F32)] * hb,
        compiler_params=_cparams("parallel", "parallel"),
    )(qt, k, vt)


def _layer_norm(y, g, b):
    mu = jnp.mean(y, axis=-1, keepdims=True)
    d = y - mu
    var = jnp.mean(d * d, axis=-1, keepdims=True)
    return d * lax.rsqrt(var + LN_EPS) * g + b


def _out_ln_kernel(*refs, n_in):
    x_ref = refs[0]
    o_refs = refs[1:1 + n_in]
    w_refs = refs[1 + n_in:1 + 2 * n_in]
    g_ref, b_ref, y_ref = refs[1 + 2 * n_in:]
    mix = _dot(o_refs[0][...].astype(BF16), w_refs[0][...])
    for o_ref, w_ref in zip(o_refs[1:], w_refs[1:]):
        mix += _dot(o_ref[...].astype(BF16), w_ref[...])
    y_ref[...] = _layer_norm(DEEPNORM_ALPHA * x_ref[...] + mix, g_ref[...], b_ref[...])


def _out_ln(x, outs, ws, g, b):
    T, D = x.shape
    tm = ROW_TILE
    n_in = len(outs)
    in_specs = [pl.BlockSpec((tm, D), lambda i: (i, 0))]
    in_specs += [pl.BlockSpec((tm, o.shape[1]), lambda i: (i, 0)) for o in outs]
    in_specs += [pl.BlockSpec(w.shape, lambda i: (0, 0)) for w in ws]
    in_specs += [pl.BlockSpec((1, D), lambda i: (0, 0))] * 2
    return pl.pallas_call(
        functools.partial(_out_ln_kernel, n_in=n_in),
        grid=(T // tm,),
        in_specs=in_specs,
        out_specs=pl.BlockSpec((tm, D), lambda i: (i, 0)),
        out_shape=jax.ShapeDtypeStruct((T, D), F32),
        compiler_params=_cparams("parallel"),
    )(x, *outs, *ws, g.reshape(1, D), b.reshape(1, D))


def _route(x, wr_ref, br_ref):
    x1 = x.astype(BF16)
    r1 = x - x1.astype(F32)
    x2 = r1.astype(BF16)
    x3 = (r1 - x2.astype(F32)).astype(BF16)
    w1, w2, w3 = wr_ref[0], wr_ref[1], wr_ref[2]
    logits = (_dot(x3, w1) + _dot(x2, w2) + _dot(x1, w3)
              + _dot(x2, w1) + _dot(x1, w2) + _dot(x1, w1) + br_ref[...])
    lane = lax.broadcasted_iota(jnp.int32, logits.shape, 1)
    is_grp = lane < N_GROUPS
    lg = jnp.where(is_grp, logits, -jnp.inf)
    mg = jnp.max(lg, axis=-1, keepdims=True)
    gidx = jnp.min(jnp.where(lg == mg, lane, LANES), axis=-1, keepdims=True)
    w_g = 1.0 / jnp.sum(jnp.where(is_grp, jnp.exp(logits - mg), 0.0), axis=-1, keepdims=True)
    first = N_GROUPS + gidx * EXPERTS_PER_GROUP
    in_grp = (lane >= first) & (lane < first + EXPERTS_PER_GROUP)
    le = jnp.where(in_grp, logits, -jnp.inf)
    v1 = jnp.max(le, axis=-1, keepdims=True)
    i1 = jnp.min(jnp.where(le == v1, lane, LANES), axis=-1, keepdims=True)
    le2 = jnp.where(lane == i1, -jnp.inf, le)
    v2 = jnp.max(le2, axis=-1, keepdims=True)
    i2 = jnp.min(jnp.where(le2 == v2, lane, LANES), axis=-1, keepdims=True)
    e2 = jnp.exp(v2 - v1)
    den = 1.0 + e2
    return jnp.where(lane == i1, (1.0 / den) * w_g, jnp.where(lane == i2, (e2 / den) * w_g, 0.0))


def _moe_kernel(x_ref, wr_ref, br_ref, wgu_ref, wd_ref, g_ref, b_ref, y_ref, comb_scr, acc_scr, xb_scr):
    x = x_ref[...]
    comb_scr[...] = _route(x, wr_ref, br_ref)
    xb_scr[...] = x.astype(BF16)
    acc_scr[...] = jnp.zeros_like(acc_scr)
    H = EXPERT_HIDDEN

    def expert(e, carry):
        xb = xb_scr[...]
        comb = comb_scr[...]
        lane = lax.broadcasted_iota(jnp.int32, comb.shape, 1)
        c = jnp.sum(jnp.where(lane == N_GROUPS + e, comb, 0.0), axis=-1, keepdims=True)
        gu = _dot(xb, wgu_ref[e])
        gate = gu[:, 0:H]
        h = (gate * (1.0 / (1.0 + jnp.exp(-gate)))) * gu[:, H:2 * H]
        acc_scr[...] += _dot((h * c).astype(BF16), wd_ref[e])
        return carry

    lax.fori_loop(0, N_EXPERTS, expert, 0)
    y_ref[...] = _layer_norm(DEEPNORM_ALPHA * x_ref[...] + acc_scr[...], g_ref[...], b_ref[...])


def _moe_ln(x, w_grp, b_grp, w_rt, b_rt, w_gate, w_up, w_down, g, b):
    T, D = x.shape
    tm = ROW_TILE
    wr = jnp.concatenate([w_grp, w_rt.transpose(1, 0, 2).reshape(D, N_EXPERTS)], axis=1)
    wr = jnp.pad(wr, ((0, 0), (0, LANES - wr.shape[1])))
    w1 = wr.astype(BF16)
    r1 = wr - w1.astype(F32)
    w2 = r1.astype(BF16)
    w3 = (r1 - w2.astype(F32)).astype(BF16)
    wr3 = jnp.stack([w1, w2, w3])
    br = jnp.pad(jnp.concatenate([b_grp, b_rt.reshape(N_EXPERTS)]), (0, LANES - N_GROUPS - N_EXPERTS))
    H = EXPERT_HIDDEN
    wgu = jnp.concatenate([w_gate, w_up], axis=2).astype(BF16)
    once = pl.Buffered(1)
    return pl.pallas_call(
        _moe_kernel,
        grid=(T // tm,),
        in_specs=[pl.BlockSpec((tm, D), lambda i: (i, 0)),
                  pl.BlockSpec((3, D, LANES), lambda i: (0, 0, 0), pipeline_mode=once),
                  pl.BlockSpec((1, LANES), lambda i: (0, 0), pipeline_mode=once),
                  pl.BlockSpec((N_EXPERTS, D, 2 * H), lambda i: (0, 0, 0), pipeline_mode=once),
                  pl.BlockSpec((N_EXPERTS, H, D), lambda i: (0, 0, 0), pipeline_mode=once),
                  pl.BlockSpec((1, D), lambda i: (0, 0), pipeline_mode=once),
                  pl.BlockSpec((1, D), lambda i: (0, 0), pipeline_mode=once)],
        out_specs=pl.BlockSpec((tm, D), lambda i: (i, 0)),
        out_shape=jax.ShapeDtypeStruct((T, D), F32),
        scratch_shapes=[pltpu.VMEM((tm, LANES), F32), pltpu.VMEM((tm, D), F32),
                        pltpu.VMEM((tm, D), BF16)],
        compiler_params=_cparams("parallel"),
    )(x, wr3, br.reshape(1, LANES), wgu, w_down.astype(BF16), g.reshape(1, D), b.reshape(1, D))


def _nsa_moba_mixer(x, tables, B, S, w_in, cmp_pos_k, cmp_pos_v, cmp_w1_k, cmp_w2_k, cmp_w1_v, cmp_w2_v):
    KV, G, H = NSA_KV_HEADS, NSA_GROUP, MOBA_HEADS
    widths = [NSA_Q_W] + [NSA_KV_W] * 6 + [NSA_GATE_W, MOBA_W, MOBA_W, MOBA_W]
    qa_w, kc_w, vc_w, ks_w, vs_w, kw_w, vw_w, ga_w, qm_w, km_w, vm_w = jnp.split(
        w_in, [int(v) for v in np.cumsum(widths)[:-1]], axis=1)
    ga_w = jnp.pad(ga_w, ((0, 0), (0, LANES - NSA_GATE_W)))
    parts = [qa_w, kc_w, ks_w, kw_w, qm_w, km_w, vc_w, vs_w, vw_w, vm_w, ga_w]
    col = [0] + [int(v) for v in np.cumsum([p.shape[1] for p in parts])]
    n_slc = S // SLC_BLOCK
    tqn = min(NSA_FLASH_Q_TILE, S)
    tqm = min(MOBA_FLASH_Q_TILE, S)
    plan = [("cols_tiles", col[0], NSA_HEADS, True, (G, tqn)),
            ("rows", col[1], KV, True, None),
            ("rows_aug", col[2], KV, True, (SLC_BLOCK, 2 * HEAD_DIM, n_slc, 2 * HEAD_DIM + n_slc)),
            ("rows", col[3], KV, True, None),
            ("cols_tiles", col[4], H, True, (1, tqm)),
            ("rows_aug", col[5], H, True, (MOBA_BLOCK, HEAD_DIM, MOBA_AUG, HEAD_DIM + MOBA_AUG)),
            ("rows", col[6], KV, False, None),
            ("cols", col[7], KV, False, V_ROWS),
            ("cols", col[8], KV, False, V_ROWS),
            ("cols", col[9], H, False, V_ROWS),
            ("sigmoid", col[10], 0, False, None)]
    (qt, kc, ks_aug, kw, qmt, km_aug, vc, vst, vwt, vmt, gates) = _project_heads(
        x, jnp.concatenate(parts, axis=1).astype(BF16), plan, B, S, tables)

    kcc = _compress(kc, cmp_pos_k, cmp_w1_k, cmp_w2_k)
    vcc = _compress(vc, cmp_pos_v, cmp_w1_v, cmp_w2_v)
    o_c, qt_aug = _nsa_compressed(qt, kcc, vcc.transpose(0, 2, 1), S)
    o_s = _flash(qt_aug, ks_aug, vst, G=G, kv_heads=1, heads_per_batch=NSA_HEADS, tk=NSA_SEL_K_TILE)
    o_w = _flash(qt, kw, vwt, G=G, kv_heads=1, heads_per_batch=NSA_HEADS, tk=NSA_WIN_K_TILE, window=WINDOW)
    o_a = _nsa_combine(o_c, o_s, o_w, gates)

    qmt_aug = _moba_gate(qmt, _block_mean(km_aug))
    o_b = _flash(qmt_aug, km_aug, vmt, G=1, kv_heads=FLASH_SPLIT, heads_per_batch=H, tk=MOBA_K_TILE)
    return o_a, o_b


def _sb_mixer(x, B, S, w_in):
    H = SB_HEADS
    plan = [("cols_tiles", 0, H, False, (1, min(SB_TILE, S))),
            ("rows", SB_W, H, False, None),
            ("cols", 2 * SB_W, H, False, HEAD_DIM)]
    qt, k, vt = _project_heads(x, w_in.astype(BF16), plan, B, S)
    return _stick_breaking(qt, k, vt, H)


def kernel(x, positions, ab_w_in, ab_w_out, nsa_cmp_pos_k, nsa_cmp_pos_v, nsa_cmp_w1_k, nsa_cmp_w2_k,
           nsa_cmp_w1_v, nsa_cmp_w2_v, sb_w_in, sb_w_out, ln_mix_g, ln_mix_b, ln_ffn_g, ln_ffn_b,
           moe_w_grp, moe_b_grp, moe_w_rt, moe_b_rt, moe_w_gate, moe_w_up, moe_w_down):
    B, S, D = x.shape
    T = B * S
    assert S % ROW_TILE == 0 and S % MOBA_BLOCK == 0 and S // MOBA_BLOCK <= MOBA_AUG
    h = x.reshape(T, D)
    tables = _rope_tables(positions.reshape(T, 1).astype(F32))
    n_layers = ln_mix_g.shape[0]
    for layer in range(n_layers):
        i = layer // 2
        if layer % 2 == 0:
            o_a, o_b = _nsa_moba_mixer(h, tables, B, S, ab_w_in[i], nsa_cmp_pos_k[i], nsa_cmp_pos_v[i],
                                       nsa_cmp_w1_k[i], nsa_cmp_w2_k[i], nsa_cmp_w1_v[i], nsa_cmp_w2_v[i])
            w_out = ab_w_out[i].astype(BF16)
            h = _out_ln(h, [o_a, o_b], [w_out[:NSA_Q_W], w_out[NSA_Q_W:]], ln_mix_g[layer], ln_mix_b[layer])
        else:
            o = _sb_mixer(h, B, S, sb_w_in[i])
            h = _out_ln(h, [o], [sb_w_out[i].astype(BF16)], ln_mix_g[layer], ln_mix_b[layer])
        h = _moe_ln(h, moe_w_grp[layer], moe_b_grp[layer], moe_w_rt[layer], moe_b_rt[layer],
                    moe_w_gate[layer], moe_w_up[layer], moe_w_down[layer], ln_ffn_g[layer], ln_ffn_b[layer])
    return h.reshape(B, S, D)
```

```python
import functools
import math

import numpy as np
import jax
import jax.numpy as jnp
from jax import lax
from jax.experimental import pallas as pl
from jax.experimental.pallas import tpu as pltpu

F32 = jnp.float32
BF16 = jnp.bfloat16

LANES = 128
VMEM_LIMIT_BYTES = 56 * 1024 * 1024

HEAD_DIM = 64
ROPE_THETA = 10000.0
LN_EPS = 1e-5

NSA_HEADS = 8
NSA_KV_HEADS = 2
NSA_GROUP = NSA_HEADS // NSA_KV_HEADS
CMP_BLOCK = 32
CMP_STRIDE = 16
CMP_HIDDEN = 2 * HEAD_DIM
SLC_BLOCK = 64
SLC_TOPN = 16
WINDOW = 512
FORCE_BONUS = 1.0e4

MOBA_HEADS = 8
MOBA_BLOCK = 256
MOBA_TOPK = 3

SB_HEADS = 16

N_GROUPS = 4
EXPERTS_PER_GROUP = 4
N_EXPERTS = N_GROUPS * EXPERTS_PER_GROUP
EXPERT_HIDDEN = 256

DEPTH = 2
DEEPNORM_ALPHA = float((2 * DEPTH) ** 0.25)

NSA_Q_W = NSA_HEADS * HEAD_DIM
NSA_KV_W = NSA_KV_HEADS * HEAD_DIM
NSA_GATE_W = 3 * NSA_HEADS
MOBA_W = MOBA_HEADS * HEAD_DIM
SB_W = SB_HEADS * HEAD_DIM

SCALE = HEAD_DIM ** -0.5
LOG2_E = 1.0 / math.log(2.0)
BF16_SUBLANES = 16
V_ROWS = -(-(HEAD_DIM + 1) // BF16_SUBLANES) * BF16_SUBLANES
FLASH_SPLIT = 2
MASK_NEG = -(2.0 ** 60)
SB_EXP_ZERO = -110.0

ROW_TILE = 512
PROJ_CHUNK = 512
NSA_Q_TILE = 128
NSA_FLASH_Q_TILE = 256
NSA_SEL_K_TILE = 512
NSA_WIN_K_TILE = 256
MOBA_FLASH_Q_TILE = 512
MOBA_K_TILE = 512
MOBA_AUG = 64
SB_TILE = 256
SB_HEADS_PER_STEP = 4


def _cparams(*sem):
    return pltpu.CompilerParams(dimension_semantics=sem, vmem_limit_bytes=VMEM_LIMIT_BYTES)


def _dot(a, b):
    return jnp.dot(a, b, preferred_element_type=F32)


def _split2(x):
    hi = x.astype(BF16)
    lo = (x - hi.astype(F32)).astype(BF16)
    return hi, lo


def _rope_table_kernel(pos_ref, inv_ref, cos_ref, sin_ref):
    ang = pos_ref[...] * inv_ref[...]
    lane = lax.broadcasted_iota(jnp.int32, ang.shape, 1)
    sign = jnp.where((lane % HEAD_DIM) < HEAD_DIM // 2, -1.0, 1.0)
    cos_ref[...] = jnp.cos(ang)
    sin_ref[...] = jnp.sin(ang) * sign


def _rope_tables(pos_f32):
    T = pos_f32.shape[0]
    half = HEAD_DIM // 2
    inv = ROPE_THETA ** (-np.arange(half, dtype=np.float64) / half)
    inv_row = jnp.asarray(np.tile(inv, LANES // half)[None, :], F32)
    tm = ROW_TILE
    return pl.pallas_call(
        _rope_table_kernel,
        grid=(T // tm,),
        in_specs=[pl.BlockSpec((tm, 1), lambda i: (i, 0)),
                  pl.BlockSpec((1, LANES), lambda i: (0, 0))],
        out_specs=[pl.BlockSpec((tm, LANES), lambda i: (i, 0))] * 2,
        out_shape=[jax.ShapeDtypeStruct((T, LANES), F32)] * 2,
        compiler_params=_cparams("parallel"),
    )(pos_f32, inv_row)


def _proj_heads_kernel(*refs, plan, tm, tiles_per_seq, with_rope):
    if with_rope:
        x_ref, w_ref, cos_ref, sin_ref = refs[:4]
        out_refs = refs[4:]
        cos = cos_ref[...]
        sin = sin_ref[...]
        lane = lax.broadcasted_iota(jnp.int32, cos.shape, 1)
        first_half = (lane % HEAD_DIM) < HEAD_DIM // 2
    else:
        x_ref, w_ref = refs[:2]
        out_refs = refs[2:]
    xb = x_ref[...].astype(BF16)
    pos = (pl.program_id(0) % tiles_per_seq) * tm + lax.broadcasted_iota(jnp.int32, (tm, 1), 0)
    chunks = {}

    def pair(c0, rope):
        k = c0 // PROJ_CHUNK
        if k not in chunks:
            lo = k * PROJ_CHUNK
            chunks[k] = _dot(xb, w_ref[:, lo:min(lo + PROJ_CHUNK, w_ref.shape[1])])
        a = chunks[k][:, c0 - k * PROJ_CHUNK:c0 - k * PROJ_CHUNK + LANES]
        if rope:
            swapped = jnp.where(first_half, pltpu.roll(a, LANES - HEAD_DIM // 2, 1),
                                pltpu.roll(a, HEAD_DIM // 2, 1))
            a = a * cos + swapped * sin
        return a

    for (kind, col0, n_heads, rope, prm), o_ref in zip(plan, out_refs):
        if kind == "sigmoid":
            o_ref[...] = 1.0 / (1.0 + jnp.exp(-pair(col0, False)))
            continue
        for h in range(0, n_heads, 2):
            a = pair(col0 + h * HEAD_DIM, rope)
            if kind in ("rows", "rows_aug"):
                for d in range(2):
                    o_ref[h + d, :, 0:HEAD_DIM] = a[:, d * HEAD_DIM:(d + 1) * HEAD_DIM].astype(o_ref.dtype)
            else:
                at = a.T
                for d in range(2):
                    head = at[d * HEAD_DIM:(d + 1) * HEAD_DIM, :].astype(o_ref.dtype)
                    if kind == "cols":
                        o_ref[h + d, 0:HEAD_DIM, :] = head
                    else:
                        G, tq = prm
                        grp, g = divmod(h + d, G)
                        if tq <= tm:
                            for u in range(tm // tq):
                                o_ref[grp, u, :, g * tq:(g + 1) * tq] = head[:, u * tq:(u + 1) * tq]
                        else:
                            o_ref[grp, 0, :, :] = head
        if kind == "rows_aug":
            block, off, width = prm
            C = o_ref.shape[2]
            onehot = jnp.where(pos // block == lax.broadcasted_iota(jnp.int32, (tm, width), 1), 1.0, 0.0)
            for h in range(n_heads):
                if off > HEAD_DIM:
                    o_ref[h, :, HEAD_DIM:off] = jnp.zeros((tm, off - HEAD_DIM), o_ref.dtype)
                o_ref[h, :, off:off + width] = onehot.astype(o_ref.dtype)
                if off + width < C:
                    o_ref[h, :, off + width:C] = jnp.zeros((tm, C - off - width), o_ref.dtype)
        if kind == "cols" and o_ref.shape[1] > HEAD_DIM:
            VR = o_ref.shape[1]
            extra = jnp.where(lax.broadcasted_iota(jnp.int32, (VR - HEAD_DIM, tm), 0) == 0, 1.0, 0.0)
            for h in range(n_heads):
                o_ref[h, HEAD_DIM:VR, :] = extra.astype(o_ref.dtype)


def _project_heads(x, w_bf16, plan, B, S, tables=None):
    T, D = x.shape
    N = w_bf16.shape[1]
    tm = ROW_TILE
    tps = S // tm
    in_specs = [pl.BlockSpec((tm, D), lambda i: (i, 0)),
                pl.BlockSpec((D, N), lambda i: (0, 0), pipeline_mode=pl.Buffered(1))]
    args = [x, w_bf16]
    if tables is not None:
        in_specs += [pl.BlockSpec((tm, LANES), lambda i: (i, 0))] * 2
        args += list(tables)
    out_specs, out_shapes = [], []
    for kind, col0, n, rope, prm in plan:
        if kind == "sigmoid":
            out_specs.append(pl.BlockSpec((tm, LANES), lambda i: (i, 0)))
            out_shapes.append(jax.ShapeDtypeStruct((T, LANES), F32))
        elif kind == "rows":
            out_specs.append(pl.BlockSpec((n, tm, HEAD_DIM), lambda i: (i // tps, i % tps, 0)))
            out_shapes.append(jax.ShapeDtypeStruct((B * n, S, HEAD_DIM), BF16))
        elif kind == "rows_aug":
            C = prm[3]
            out_specs.append(pl.BlockSpec((n, tm, C), lambda i: (i // tps, i % tps, 0)))
            out_shapes.append(jax.ShapeDtypeStruct((B * n, S, C), BF16))
        elif kind == "cols":
            VR = prm
            out_specs.append(pl.BlockSpec((n, VR, tm), lambda i: (i // tps, 0, i % tps)))
            out_shapes.append(jax.ShapeDtypeStruct((B * n, VR, S), BF16))
        else:
            G, tq = prm
            ng = n // G
            if tq <= tm:
                out_specs.append(pl.BlockSpec((ng, tm // tq, HEAD_DIM, G * tq), lambda i: (i // tps, i % tps, 0, 0)))
            else:
                r = tq // tm
                out_specs.append(pl.BlockSpec((ng, 1, HEAD_DIM, tm),
                                              lambda i, r=r: (i // tps, (i % tps) // r, 0, (i % tps) % r)))
            out_shapes.append(jax.ShapeDtypeStruct((B * ng, S // tq, HEAD_DIM, G * tq), BF16))
    kplan = tuple((k, c, n, r, (p[:3] if k == "rows_aug" else p)) for k, c, n, r, p in plan)
    return pl.pallas_call(
        functools.partial(_proj_heads_kernel, plan=kplan, tm=tm, tiles_per_seq=tps, with_rope=tables is not None),
        grid=(T // tm,),
        in_specs=in_specs,
        out_specs=out_specs,
        out_shape=out_shapes,
        compiler_params=_cparams("parallel"),
    )(*args)


def _compress_kernel(kv_ref, pos_ref, w1_ref, w2_ref, o_ref):
    kv = kv_ref[0].astype(F32)
    n = kv.shape[0]
    half = CMP_STRIDE * HEAD_DIM
    first = _dot((kv + pos_ref[0:1, :]).astype(BF16), w1_ref[0:half, :])
    second = _dot((kv + pos_ref[1:2, :]).astype(BF16), w1_ref[half:2 * half, :])
    h = first + pltpu.roll(second, n - 1, 0)
    g = 0.5 * h * (1.0 + jnp.tanh(math.sqrt(2.0 / math.pi) * (h + 0.044715 * (h * h * h))))
    o_ref[0] = _dot(g.astype(BF16), w2_ref[...]).astype(o_ref.dtype)


def _compress(kv, pos_emb, w1, w2):
    NB, S, _ = kv.shape
    n = S // CMP_STRIDE
    half = CMP_STRIDE * HEAD_DIM
    kvr = kv.reshape(NB, n, half)
    pos2 = pos_emb.reshape(2, half)
    return pl.pallas_call(
        _compress_kernel,
        grid=(NB,),
        in_specs=[pl.BlockSpec((1, n, half), lambda b: (b, 0, 0)),
                  pl.BlockSpec((2, half), lambda b: (0, 0)),
                  pl.BlockSpec((2 * half, CMP_HIDDEN), lambda b: (0, 0)),
                  pl.BlockSpec((CMP_HIDDEN, HEAD_DIM), lambda b: (0, 0))],
        out_specs=pl.BlockSpec((1, n, HEAD_DIM), lambda b: (b, 0, 0)),
        out_shape=jax.ShapeDtypeStruct((NB, n, HEAD_DIM), BF16),
        compiler_params=_cparams("parallel"),
    )(kvr, pos2, w1.astype(BF16), w2.astype(BF16))


def _store_heads_token_major(o_ref, out_t, tq, first_head):
    for g in range(0, out_t.shape[1] // tq, 2):
        pair = jnp.concatenate([out_t[:, g * tq:(g + 1) * tq], out_t[:, (g + 1) * tq:(g + 2) * tq]], axis=0)
        c0 = (first_head + g) * HEAD_DIM
        o_ref[:, c0:c0 + 2 * HEAD_DIM] = pair.T.astype(o_ref.dtype)


def _first_max_pick_t(score, idx, height):
    m = jnp.max(score, axis=0, keepdims=True)
    first = jnp.min(jnp.where(score == m, idx, height), axis=0, keepdims=True)
    return idx == first


def _nsa_cmp_kernel(qt_ref, kc_ref, vct_ref, oc_ref, qaug_ref, *, tq, n_slc, top_n):
    G = NSA_GROUP
    R = G * tq
    q0 = pl.program_id(1) * tq
    qt = qt_ref[0, 0]
    kc = kc_ref[0]
    ncp = kc.shape[0]
    s = _dot(kc, qt * SCALE)
    t_row = q0 + (lax.broadcasted_iota(jnp.int32, (1, R), 1) % tq)
    cmp_end = lax.broadcasted_iota(jnp.int32, (ncp, 1), 0) * CMP_STRIDE + (CMP_BLOCK - 1)
    s = jnp.where(cmp_end <= t_row, s, -jnp.inf)
    m = jnp.max(s, axis=0, keepdims=True)
    m = jnp.where(m == -jnp.inf, 0.0, m)
    e = jnp.exp(s - m)
    p = e * (1.0 / jnp.maximum(jnp.sum(e, axis=0, keepdims=True), 1e-30))
    _store_heads_token_major(oc_ref, _dot(vct_ref[0], p.astype(BF16)), tq, 0)

    pg = p[:, 0:tq]
    for g in range(1, G):
        pg = pg + p[:, g * tq:(g + 1) * tq]
    sj = lax.broadcasted_iota(jnp.int32, (n_slc, ncp), 0)
    ci = lax.broadcasted_iota(jnp.int32, (n_slc, ncp), 1)
    overlap = jnp.where(ci * CMP_STRIDE < (sj + 1) * SLC_BLOCK,
                        jnp.where(ci * CMP_STRIDE + CMP_BLOCK - 1 >= sj * SLC_BLOCK, 1.0, 0.0), 0.0)
    overlap = overlap.astype(BF16)
    p_hi, p_lo = _split2(pg)
    imp = _dot(overlap, p_hi) + _dot(overlap, p_lo)

    blk = lax.broadcasted_iota(jnp.int32, (n_slc, tq), 0)
    cur = (q0 + lax.broadcasted_iota(jnp.int32, (1, tq), 1)) // SLC_BLOCK
    forced = (blk == 0) | (blk == cur) | (blk == cur - 1)
    valid = blk <= cur
    score = jnp.where(forced, imp + FORCE_BONUS, jnp.where(valid, imp, -1.0))

    def pick(_, carry):
        score, sel = carry
        hit = _first_max_pick_t(score, blk, n_slc)
        return jnp.where(hit, -jnp.inf, score), jnp.where(hit, 1.0, sel)

    _, sel = lax.fori_loop(0, top_n, pick, (score, jnp.zeros((n_slc, tq), F32)))
    neg = jnp.where(valid, jnp.where(sel > 0.0, 0.0, MASK_NEG), MASK_NEG).astype(BF16)

    qaug_ref[0, 0, 0:HEAD_DIM, :] = qt
    qaug_ref[0, 0, HEAD_DIM:2 * HEAD_DIM, :] = jnp.zeros((HEAD_DIM, R), BF16)
    for g in range(G):
        qaug_ref[0, 0, 2 * HEAD_DIM:2 * HEAD_DIM + n_slc, g * tq:(g + 1) * tq] = neg


def _token_major_spec(tq, G, n_tiles, groups_per_batch):
    return pl.BlockSpec((tq, G * HEAD_DIM),
                        lambda b, i: ((b // groups_per_batch) * n_tiles + i, b % groups_per_batch))


def _nsa_compressed(qt, kc, vct, S):
    NB, n_tiles, _, R = qt.shape
    G = NSA_GROUP
    tq = R // G
    ncp = kc.shape[1]
    n_slc = S // SLC_BLOCK
    C = 2 * HEAD_DIM + n_slc
    kern = functools.partial(_nsa_cmp_kernel, tq=tq, n_slc=n_slc, top_n=min(SLC_TOPN, n_slc))
    return pl.pallas_call(
        kern,
        grid=(NB, n_tiles),
        in_specs=[pl.BlockSpec((1, 1, HEAD_DIM, R), lambda b, i: (b, i, 0, 0)),
                  pl.BlockSpec((1, ncp, HEAD_DIM), lambda b, i: (b, 0, 0)),
                  pl.BlockSpec((1, HEAD_DIM, ncp), lambda b, i: (b, 0, 0))],
        out_specs=[_token_major_spec(tq, G, n_tiles, NSA_KV_HEADS),
                   pl.BlockSpec((1, 1, C, R), lambda b, i: (b, i, 0, 0))],
        out_shape=[jax.ShapeDtypeStruct((NB * S // NSA_KV_HEADS, NSA_Q_W), BF16),
                   jax.ShapeDtypeStruct((NB, n_tiles, C, R), BF16)],
        compiler_params=_cparams("parallel", "parallel"),
    )(qt, kc, vct)


def _flash_kernel(qt_ref, k_ref, vt_ref, o_ref, *scratch, G, tq, tk, window, n_split, kv_heads):
    R = G * tq * kv_heads
    W = R // n_split
    VR = vt_ref.shape[1]
    q0 = pl.program_id(1) * tq
    q_scrs, s_scrs0, s_scrs1, p_scrs, acc_scrs = (scratch[i * n_split:(i + 1) * n_split] for i in range(5))
    s_bufs = (s_scrs0, s_scrs1)
    t_all = q0 + (lax.broadcasted_iota(jnp.int32, (1, R), 1) % tq)
    for h in range(n_split):
        q_h = qt_ref[0, 0, :, h * W:(h + 1) * W] if kv_heads == 1 else qt_ref[h, 0]
        q_scrs[h][...] = (q_h.astype(F32) * (SCALE * LOG2_E)).astype(BF16)
        acc_scrs[h][...] = jnp.zeros((VR, W), F32)

    def group(h, slot, vt_tile, k0, m, masked):
        s_scr, p_scr, acc_scr = s_bufs[slot][h], p_scrs[h], acc_scrs[h]
        t_row = t_all[:, h * W:(h + 1) * W]

        def load(r0, rows):
            s = s_scr[pl.ds(r0, rows), :]
            if masked:
                kpos = k0 + r0 + lax.broadcasted_iota(jnp.int32, (rows, 1), 0)
                if window is not None:
                    s = jnp.where(t_row - kpos < window, s, -jnp.inf)
                s = jnp.where(kpos <= t_row, s, -jnp.inf)
            return s

        def col_max(i, m8):
            return jnp.maximum(m8, load(pl.multiple_of(i * 8, 8), 8))

        m8 = lax.fori_loop(0, tk // 8, col_max, jnp.full((8, W), -jnp.inf, F32), unroll=True)
        m_new = jnp.maximum(m, jnp.max(m8, axis=0, keepdims=True))
        m_safe = jnp.where(m_new == -jnp.inf, 0.0, m_new)
        alpha = jnp.exp2(m - m_safe)

        def probs(i, carry):
            r0 = pl.multiple_of(i * 16, 16)
            p_scr[pl.ds(r0, 16), :] = jnp.exp2(load(r0, 16) - m_safe).astype(BF16)
            return carry

        lax.fori_loop(0, tk // 16, probs, 0, unroll=True)
        acc_scr[...] = alpha * acc_scr[...] + _dot(vt_tile, p_scr[...])
        return m_new

    def scores(j, slot):
        k0 = pl.multiple_of(j * tk, tk)
        for h in range(n_split):
            k_tile = k_ref[h if kv_heads > 1 else 0, pl.ds(k0, tk), :]
            s_bufs[slot][h][...] = _dot(k_tile, q_scrs[h][...])

    def softmax_pv(j, slot, ms, masked):
        k0 = pl.multiple_of(j * tk, tk)
        return tuple(group(h, slot, vt_ref[h if kv_heads > 1 else 0, :, pl.ds(k0, tk)], k0, ms[h], masked)
                     for h in range(n_split))

    def tile(j, ms):
        scores(j, 0)
        return softmax_pv(j, 0, ms, True)

    ms = tuple(jnp.full((1, W), -jnp.inf, F32) for _ in range(n_split))
    j_hi = (q0 + tq - 1) // tk
    if window is None:
        n_pairs = (q0 // tk) // 2

        def pair(i, ms):
            scores(2 * i + 1, 1)
            ms = softmax_pv(2 * i, 0, ms, False)
            scores(2 * i + 2, 0)
            return softmax_pv(2 * i + 1, 1, ms, False)

        scores(0, 0)
        ms = lax.fori_loop(0, n_pairs, pair, ms)
        leftover = j_hi - 2 * n_pairs

        @pl.when(leftover == 0)
        def _():
            softmax_pv(j_hi, 0, ms, True)

        @pl.when(leftover == 1)
        def _():
            scores(j_hi, 1)
            softmax_pv(j_hi, 1, softmax_pv(j_hi - 1, 0, ms, False), True)
    elif window == 2 * tk and tq == tk:
        @pl.when(q0 >= window)
        def _():
            scores(j_hi - 2, 0)
            scores(j_hi - 1, 1)
            m1 = softmax_pv(j_hi - 2, 0, ms, True)
            scores(j_hi, 0)
            softmax_pv(j_hi, 0, softmax_pv(j_hi - 1, 1, m1, False), True)

        @pl.when(q0 < window)
        def _():
            lax.fori_loop(0, j_hi + 1, tile, ms)
    else:
        j_lo = jnp.maximum(q0 - (window - 1), 0) // tk
        lax.fori_loop(j_lo, j_hi + 1, tile, ms)
    outs = []
    for h in range(n_split):
        acc = acc_scrs[h][...]
        outs.append(acc[0:HEAD_DIM, :] / acc[HEAD_DIM:HEAD_DIM + 1, :])
    _store_heads_token_major(o_ref, jnp.concatenate(outs, axis=1), tq, 0)


def _flash(qt, k, vt, *, G, kv_heads, heads_per_batch, tk, window=None):
    NB, n_tiles, C, GW = qt.shape
    S = k.shape[1]
    tq = GW // G
    tk = min(tk, S)
    ns = FLASH_SPLIT
    assert kv_heads in (1, ns) and (kv_heads == 1 or G == 1) and tq <= tk
    W = GW * kv_heads // ns
    per_step = G * kv_heads
    out_spec = _token_major_spec(tq, per_step, n_tiles, heads_per_batch // per_step)
    out_shape = jax.ShapeDtypeStruct((NB * G // heads_per_batch * S, heads_per_batch * HEAD_DIM), BF16)
    kern = functools.partial(_flash_kernel, G=G, tq=tq, tk=tk, window=window, n_split=ns, kv_heads=kv_heads)
    return pl.pallas_call(
        kern,
        grid=(NB // kv_heads, S // tq),
        in_specs=[pl.BlockSpec((kv_heads, 1, C, GW), lambda b, i: (b, i, 0, 0)),
                  pl.BlockSpec((kv_heads, S, C), lambda b, i: (b, 0, 0)),
                  pl.BlockSpec((kv_heads, V_ROWS, S), lambda b, i: (b, 0, 0))],
        out_specs=out_spec,
        out_shape=out_shape,
        scratch_shapes=([pltpu.VMEM((C, W), BF16)] * ns + [pltpu.VMEM((tk, W), F32)] * (2 * ns)
                        + [pltpu.VMEM((tk, W), BF16)] * ns + [pltpu.VMEM((V_ROWS, W), F32)] * ns),
        compiler_params=_cparams("parallel", "parallel"),
    )(qt, k, vt)


def _block_mean_kernel(k_ref, o_ref, *, n_blk):
    k = k_ref[0][:, 0:HEAD_DIM].astype(F32).reshape(n_blk, MOBA_BLOCK, HEAD_DIM)
    o_ref[0] = (jnp.sum(k, axis=1) * (1.0 / MOBA_BLOCK)).astype(o_ref.dtype)


def _block_mean(k):
    NB, S, C = k.shape
    n_blk = S // MOBA_BLOCK
    return pl.pallas_call(
        functools.partial(_block_mean_kernel, n_blk=n_blk),
        grid=(NB,),
        in_specs=[pl.BlockSpec((1, S, C), lambda b: (b, 0, 0))],
        out_specs=pl.BlockSpec((1, n_blk, HEAD_DIM), lambda b: (b, 0, 0)),
        out_shape=jax.ShapeDtypeStruct((NB, n_blk, HEAD_DIM), BF16),
        compiler_params=_cparams("parallel"),
    )(k)


def _moba_gate_kernel(qt_ref, km_ref, qaug_ref, *, tq, n_blk, top_k):
    q0 = pl.program_id(1) * tq
    qt = qt_ref[0, 0]
    gate = _dot(km_ref[0], qt)
    blk = lax.broadcasted_iota(jnp.int32, (n_blk, tq), 0)
    cur = (q0 + lax.broadcasted_iota(jnp.int32, (1, tq), 1)) // MOBA_BLOCK
    past = blk < cur
    gate = jnp.where(past, gate, -jnp.inf)
    taken = jnp.zeros((n_blk, tq), F32)
    for _ in range(top_k):
        hit = _first_max_pick_t(jnp.where(taken > 0.0, -jnp.inf, gate), blk, n_blk)
        taken = jnp.where(hit, 1.0, taken)
    neg = jnp.where(blk == cur, 0.0,
                    jnp.where(past, jnp.where(taken > 0.0, 0.0, MASK_NEG), MASK_NEG)).astype(BF16)
    qaug_ref[0, 0, 0:HEAD_DIM, :] = qt
    qaug_ref[0, 0, HEAD_DIM:HEAD_DIM + n_blk, :] = neg
    if n_blk < MOBA_AUG:
        qaug_ref[0, 0, HEAD_DIM + n_blk:, :] = jnp.zeros((MOBA_AUG - n_blk, tq), BF16)


def _moba_gate(qt, kmean):
    NB, n_tiles, _, tq = qt.shape
    n_blk = kmean.shape[1]
    C = HEAD_DIM + MOBA_AUG
    kern = functools.partial(_moba_gate_kernel, tq=tq, n_blk=n_blk, top_k=min(MOBA_TOPK, n_blk))
    return pl.pallas_call(
        kern,
        grid=(NB, n_tiles),
        in_specs=[pl.BlockSpec((1, 1, HEAD_DIM, tq), lambda b, i: (b, i, 0, 0)),
                  pl.BlockSpec((1, n_blk, HEAD_DIM), lambda b, i: (b, 0, 0))],
        out_specs=pl.BlockSpec((1, 1, C, tq), lambda b, i: (b, i, 0, 0)),
        out_shape=jax.ShapeDtypeStruct((NB, n_tiles, C, tq), BF16),
        compiler_params=_cparams("parallel", "parallel"),
    )(qt, kmean)


def _sb_kernel(qt_ref, k_ref, vt_ref, o_ref, *acc_scrs, tile, hb):
    qi = pl.program_id(1)
    q0 = qi * tile
    t_row = q0 + lax.broadcasted_iota(jnp.int32, (1, tile), 1)
    later = (lax.broadcasted_iota(jnp.int32, (tile, tile), 1)
             > lax.broadcasted_iota(jnp.int32, (tile, tile), 0))
    later = jnp.where(later, 1.0, 0.0).astype(BF16)
    for acc in acc_scrs:
        acc[...] = jnp.zeros((HEAD_DIM, tile), F32)

    def step(j, carries, masked):
        k0 = pl.multiple_of(j * tile, tile)
        zs = [_dot(k_ref[h, pl.ds(k0, tile), :], qt_ref[h, 0] * SCALE) for h in range(hb)]
        if masked:
            mask = (k0 + lax.broadcasted_iota(jnp.int32, (tile, 1), 0)) < t_row
        sps = [jnp.maximum(z, 0.0) + jnp.log(1.0 + jnp.exp(-jnp.abs(z))) for z in zs]
        log_1ms = [jnp.where(mask, -sp, 0.0) if masked else -sp for sp in sps]
        parts = [_split2(x) for x in log_1ms]
        betweens = [_dot(later, hi) + _dot(later, lo) + c for (hi, lo), c in zip(parts, carries)]
        new_carries = []
        for h in range(hb):
            w = jnp.exp((zs[h] - sps[h]) + betweens[h])
            if masked:
                w = jnp.where(mask, w, 0.0)
            acc_scrs[h][...] += _dot(vt_ref[h, :, pl.ds(k0, tile)], w.astype(BF16))
            new_carries.append(carries[h] + jnp.sum(log_1ms[h], axis=0, keepdims=True))
        worst = new_carries[0]
        for c in new_carries[1:]:
            worst = jnp.maximum(worst, c)
        return tuple(new_carries), jnp.max(worst)

    carries, worst = step(qi, tuple(jnp.zeros((1, tile), F32) for _ in range(hb)), True)

    def cond(state):
        j, _, worst = state
        return (j >= 0) & (worst > SB_EXP_ZERO)

    def body(state):
        j, carries, _ = state
        carries, worst = step(j, carries, False)
        return j - 1, carries, worst

    lax.while_loop(cond, body, (qi - 1, carries, worst))
    for h in range(0, hb, 2):
        pair = jnp.concatenate([acc_scrs[h][...], acc_scrs[h + 1][...]], axis=0)
        o_ref[:, h * HEAD_DIM:(h + 2) * HEAD_DIM] = pair.T.astype(o_ref.dtype)


def _stick_breaking(qt, k, vt, n_heads):
    NB, S, _ = k.shape
    tile = qt.shape[3]
    hb = SB_HEADS_PER_STEP
    steps_per_batch = n_heads // hb
    return pl.pallas_call(
        functools.partial(_sb_kernel, tile=tile, hb=hb),
        grid=(NB // hb, S // tile),
        in_specs=[pl.BlockSpec((hb, 1, HEAD_DIM, tile), lambda b, i: (b, i, 0, 0)),
                  pl.BlockSpec((hb, S, HEAD_DIM), lambda b, i: (b, 0, 0)),
                  pl.BlockSpec((hb, HEAD_DIM, S), lambda b, i: (b, 0, 0))],
        out_specs=pl.BlockSpec((tile, hb * HEAD_DIM),
                               lambda b, i: ((b // steps_per_batch) * (S // tile) + i, b % steps_per_batch)),
        out_shape=jax.ShapeDtypeStruct((NB // n_heads * S, n_heads * HEAD_DIM), BF16),
        scratch_shapes=[pltpu.VMEM((HEAD_DIM, tile), F32)] * hb,
        compiler_params=_cparams("parallel", "parallel"),
    )(qt, k, vt)


def _layer_norm(y, g, b):
    mu = jnp.mean(y, axis=-1, keepdims=True)
    d = y - mu
    var = jnp.mean(d * d, axis=-1, keepdims=True)
    return d * lax.rsqrt(var + LN_EPS) * g + b


def _out_ln_kernel(*refs, n_branch, n_plain):
    it = iter(refs)
    x_ref = next(it)
    br_refs = [next(it) for _ in range(n_branch)]
    gate_ref = next(it) if n_branch else None
    o_refs = [next(it) for _ in range(n_plain)]
    w_refs = [next(it) for _ in range(n_plain + (1 if n_branch else 0))]
    g_ref, b_ref, y_ref = next(it), next(it), next(it)
    mix = None
    if n_branch:
        mixed_scr = next(it)
        gate = gate_ref[...]
        for h in range(NSA_HEADS):
            sl = slice(h * HEAD_DIM, (h + 1) * HEAD_DIM)
            o = gate[:, n_branch * h:n_branch * h + 1] * br_refs[0][:, sl]
            for c in range(1, n_branch):
                o = o + gate[:, n_branch * h + c:n_branch * h + c + 1] * br_refs[c][:, sl]
            mixed_scr[:, sl] = o.astype(BF16)
        mix = _dot(mixed_scr[...], w_refs[0][...])
        w_refs = w_refs[1:]
    for o_ref, w_ref in zip(o_refs, w_refs):
        part = _dot(o_ref[...].astype(BF16), w_ref[...])
        mix = part if mix is None else mix + part
    y_ref[...] = _layer_norm(DEEPNORM_ALPHA * x_ref[...] + mix, g_ref[...], b_ref[...])


def _out_ln(x, outs, ws, g, b, nsa=None):
    T, D = x.shape
    tm = ROW_TILE
    row = lambda a: pl.BlockSpec((tm, a.shape[1]), lambda i: (i, 0))
    const = lambda a: pl.BlockSpec(a.shape, lambda i: (0, 0), pipeline_mode=pl.Buffered(1))
    args, in_specs, scratch, n_branch = [x], [row(x)], [], 0
    ws = list(ws)
    if nsa is not None:
        branches, gates, w_nsa = nsa
        n_branch = len(branches)
        args += list(branches) + [gates]
        in_specs += [row(a) for a in branches] + [row(gates)]
        scratch = [pltpu.VMEM((tm, NSA_Q_W), BF16)]
        ws = [w_nsa] + ws
    gb = [g.reshape(1, D), b.reshape(1, D)]
    args += list(outs) + ws + gb
    in_specs += [row(o) for o in outs] + [const(w) for w in ws] + [const(a) for a in gb]
    return pl.pallas_call(
        functools.partial(_out_ln_kernel, n_branch=n_branch, n_plain=len(outs)),
        grid=(T // tm,),
        in_specs=in_specs,
        out_specs=pl.BlockSpec((tm, D), lambda i: (i, 0)),
        out_shape=jax.ShapeDtypeStruct((T, D), F32),
        scratch_shapes=scratch,
        compiler_params=_cparams("parallel"),
    )(*args)


def _route(x, wr_ref, br_ref):
    x1, x2 = _split2(x)
    w1, w2 = wr_ref[0], wr_ref[1]
    logits = _dot(x2, w1) + _dot(x1, w2) + _dot(x1, w1) + br_ref[...]
    lane = lax.broadcasted_iota(jnp.int32, logits.shape, 1)
    is_grp = lane < N_GROUPS
    lg = jnp.where(is_grp, logits, -jnp.inf)
    mg = jnp.max(lg, axis=-1, keepdims=True)
    gidx = jnp.min(jnp.where(lg == mg, lane, LANES), axis=-1, keepdims=True)
    w_g = 1.0 / jnp.sum(jnp.where(is_grp, jnp.exp(logits - mg), 0.0), axis=-1, keepdims=True)
    first = N_GROUPS + gidx * EXPERTS_PER_GROUP
    in_grp = (lane >= first) & (lane < first + EXPERTS_PER_GROUP)
    le = jnp.where(in_grp, logits, -jnp.inf)
    v1 = jnp.max(le, axis=-1, keepdims=True)
    i1 = jnp.min(jnp.where(le == v1, lane, LANES), axis=-1, keepdims=True)
    le2 = jnp.where(lane == i1, -jnp.inf, le)
    v2 = jnp.max(le2, axis=-1, keepdims=True)
    i2 = jnp.min(jnp.where(le2 == v2, lane, LANES), axis=-1, keepdims=True)
    e2 = jnp.exp(v2 - v1)
    den = 1.0 + e2
    return jnp.where(lane == i1, (1.0 / den) * w_g, jnp.where(lane == i2, (e2 / den) * w_g, 0.0))


def _moe_kernel(x_ref, wr_ref, br_ref, wgu_ref, wd_ref, g_ref, b_ref, y_ref, comb_scr, acc_scr, xb_scr):
    x = x_ref[...]
    comb_scr[...] = _route(x, wr_ref, br_ref)
    xb_scr[...] = x.astype(BF16)
    acc_scr[...] = jnp.zeros_like(acc_scr)
    H = EXPERT_HIDDEN

    def expert(e, carry):
        xb = xb_scr[...]
        comb = comb_scr[...]
        lane = lax.broadcasted_iota(jnp.int32, comb.shape, 1)
        c = jnp.sum(jnp.where(lane == N_GROUPS + e, comb, 0.0), axis=-1, keepdims=True)
        gu = _dot(xb, wgu_ref[e])
        gate = gu[:, 0:H]
        h = (gate * (1.0 / (1.0 + jnp.exp(-gate)))) * gu[:, H:2 * H]
        acc_scr[...] += _dot((h * c).astype(BF16), wd_ref[e])
        return carry

    lax.fori_loop(0, N_EXPERTS, expert, 0)
    y_ref[...] = _layer_norm(DEEPNORM_ALPHA * x_ref[...] + acc_scr[...], g_ref[...], b_ref[...])


def _moe_ln(x, w_grp, b_grp, w_rt, b_rt, w_gate, w_up, w_down, g, b):
    T, D = x.shape
    tm = ROW_TILE
    wr = jnp.concatenate([w_grp, w_rt.transpose(1, 0, 2).reshape(D, N_EXPERTS)], axis=1)
    wr = jnp.pad(wr, ((0, 0), (0, LANES - wr.shape[1])))
    wr2 = jnp.stack(_split2(wr))
    br = jnp.pad(jnp.concatenate([b_grp, b_rt.reshape(N_EXPERTS)]), (0, LANES - N_GROUPS - N_EXPERTS))
    H = EXPERT_HIDDEN
    wgu = jnp.concatenate([w_gate, w_up], axis=2).astype(BF16)
    once = pl.Buffered(1)
    return pl.pallas_call(
        _moe_kernel,
        grid=(T // tm,),
        in_specs=[pl.BlockSpec((tm, D), lambda i: (i, 0)),
                  pl.BlockSpec((2, D, LANES), lambda i: (0, 0, 0), pipeline_mode=once),
                  pl.BlockSpec((1, LANES), lambda i: (0, 0), pipeline_mode=once),
                  pl.BlockSpec((N_EXPERTS, D, 2 * H), lambda i: (0, 0, 0), pipeline_mode=once),
                  pl.BlockSpec((N_EXPERTS, H, D), lambda i: (0, 0, 0), pipeline_mode=once),
                  pl.BlockSpec((1, D), lambda i: (0, 0), pipeline_mode=once),
                  pl.BlockSpec((1, D), lambda i: (0, 0), pipeline_mode=once)],
        out_specs=pl.BlockSpec((tm, D), lambda i: (i, 0)),
        out_shape=jax.ShapeDtypeStruct((T, D), F32),
        scratch_shapes=[pltpu.VMEM((tm, LANES), F32), pltpu.VMEM((tm, D), F32),
                        pltpu.VMEM((tm, D), BF16)],
        compiler_params=_cparams("parallel"),
    )(x, wr2, br.reshape(1, LANES), wgu, w_down.astype(BF16), g.reshape(1, D), b.reshape(1, D))


def _nsa_moba_mixer(x, tables, B, S, w_in, cmp_pos_k, cmp_pos_v, cmp_w1_k, cmp_w2_k, cmp_w1_v, cmp_w2_v):
    KV, G, H = NSA_KV_HEADS, NSA_GROUP, MOBA_HEADS
    widths = [NSA_Q_W] + [NSA_KV_W] * 6 + [NSA_GATE_W, MOBA_W, MOBA_W, MOBA_W]
    qa_w, kc_w, vc_w, ks_w, vs_w, kw_w, vw_w, ga_w, qm_w, km_w, vm_w = jnp.split(
        w_in, [int(v) for v in np.cumsum(widths)[:-1]], axis=1)
    ga_w = jnp.pad(ga_w, ((0, 0), (0, LANES - NSA_GATE_W)))
    parts = [qa_w, kc_w, ks_w, kw_w, qm_w, km_w, vc_w, vs_w, vw_w, vm_w, ga_w]
    col = [0] + [int(v) for v in np.cumsum([p.shape[1] for p in parts])]
    n_slc = S // SLC_BLOCK
    tqn = min(NSA_FLASH_Q_TILE, S)
    tqm = min(MOBA_FLASH_Q_TILE, S)
    plan = [("cols_tiles", col[0], NSA_HEADS, True, (G, tqn)),
            ("rows", col[1], KV, True, None),
            ("rows_aug", col[2], KV, True, (SLC_BLOCK, 2 * HEAD_DIM, n_slc, 2 * HEAD_DIM + n_slc)),
            ("rows", col[3], KV, True, None),
            ("cols_tiles", col[4], H, True, (1, tqm)),
            ("rows_aug", col[5], H, True, (MOBA_BLOCK, HEAD_DIM, MOBA_AUG, HEAD_DIM + MOBA_AUG)),
            ("rows", col[6], KV, False, None),
            ("cols", col[7], KV, False, V_ROWS),
            ("cols", col[8], KV, False, V_ROWS),
            ("cols", col[9], H, False, V_ROWS),
            ("sigmoid", col[10], 0, False, None)]
    (qt, kc, ks_aug, kw, qmt, km_aug, vc, vst, vwt, vmt, gates) = _project_heads(
        x, jnp.concatenate(parts, axis=1).astype(BF16), plan, B, S, tables)

    kcc = _compress(kc, cmp_pos_k, cmp_w1_k, cmp_w2_k)
    vcc = _compress(vc, cmp_pos_v, cmp_w1_v, cmp_w2_v)
    o_c, qt_aug = _nsa_compressed(qt, kcc, vcc.transpose(0, 2, 1), S)
    o_s = _flash(qt_aug, ks_aug, vst, G=G, kv_heads=1, heads_per_batch=NSA_HEADS, tk=NSA_SEL_K_TILE)
    o_w = _flash(qt, kw, vwt, G=G, kv_heads=1, heads_per_batch=NSA_HEADS, tk=NSA_WIN_K_TILE, window=WINDOW)

    qmt_aug = _moba_gate(qmt, _block_mean(km_aug))
    o_b = _flash(qmt_aug, km_aug, vmt, G=1, kv_heads=FLASH_SPLIT, heads_per_batch=H, tk=MOBA_K_TILE)
    return (o_c, o_s, o_w), gates, o_b


def _sb_mixer(x, B, S, w_in):
    H = SB_HEADS
    plan = [("cols_tiles", 0, H, False, (1, min(SB_TILE, S))),
            ("rows", SB_W, H, False, None),
            ("cols", 2 * SB_W, H, False, HEAD_DIM)]
    qt, k, vt = _project_heads(x, w_in.astype(BF16), plan, B, S)
    return _stick_breaking(qt, k, vt, H)


def kernel(x, positions, ab_w_in, ab_w_out, nsa_cmp_pos_k, nsa_cmp_pos_v, nsa_cmp_w1_k, nsa_cmp_w2_k,
           nsa_cmp_w1_v, nsa_cmp_w2_v, sb_w_in, sb_w_out, ln_mix_g, ln_mix_b, ln_ffn_g, ln_ffn_b,
           moe_w_grp, moe_b_grp, moe_w_rt, moe_b_rt, moe_w_gate, moe_w_up, moe_w_down):
    B, S, D = x.shape
    T = B * S
    assert S % ROW_TILE == 0 and S % MOBA_BLOCK == 0 and S // MOBA_BLOCK <= MOBA_AUG
    h = x.reshape(T, D)
    tables = _rope_tables(positions.reshape(T, 1).astype(F32))
    n_layers = ln_mix_g.shape[0]
    for layer in range(n_layers):
        i = layer // 2
        if layer % 2 == 0:
            branches, gates, o_b = _nsa_moba_mixer(
                h, tables, B, S, ab_w_in[i], nsa_cmp_pos_k[i], nsa_cmp_pos_v[i],
                nsa_cmp_w1_k[i], nsa_cmp_w2_k[i], nsa_cmp_w1_v[i], nsa_cmp_w2_v[i])
            w_out = ab_w_out[i].astype(BF16)
            h = _out_ln(h, [o_b], [w_out[NSA_Q_W:]], ln_mix_g[layer], ln_mix_b[layer],
                        nsa=(branches, gates, w_out[:NSA_Q_W]))
        else:
            o = _sb_mixer(h, B, S, sb_w_in[i])
            h = _out_ln(h, [o], [sb_w_out[i].astype(BF16)], ln_mix_g[layer], ln_mix_b[layer])
        h = _moe_ln(h, moe_w_grp[layer], moe_b_grp[layer], moe_w_rt[layer], moe_b_rt[layer],
                    moe_w_gate[layer], moe_w_up[layer], moe_w_down[layer], ln_ffn_g[layer], ln_ffn_b[layer])
    return h.reshape(B, S, D)
```

```python
import functools
import math

import numpy as np
import jax
import jax.numpy as jnp
from jax import lax
from jax.experimental import pallas as pl
from jax.experimental.pallas import tpu as pltpu

F32 = jnp.float32
BF16 = jnp.bfloat16

LANES = 128
VMEM_LIMIT_BYTES = 56 * 1024 * 1024

HEAD_DIM = 64
ROPE_THETA = 10000.0
LN_EPS = 1e-5

NSA_HEADS = 8
NSA_KV_HEADS = 2
NSA_GROUP = NSA_HEADS // NSA_KV_HEADS
CMP_BLOCK = 32
CMP_STRIDE = 16
CMP_HIDDEN = 2 * HEAD_DIM
SLC_BLOCK = 64
SLC_TOPN = 16
WINDOW = 512
FORCE_BONUS = 1.0e4

MOBA_HEADS = 8
MOBA_BLOCK = 256
MOBA_TOPK = 3

SB_HEADS = 16

N_GROUPS = 4
EXPERTS_PER_GROUP = 4
N_EXPERTS = N_GROUPS * EXPERTS_PER_GROUP
EXPERT_HIDDEN = 256

DEPTH = 2
DEEPNORM_ALPHA = float((2 * DEPTH) ** 0.25)

NSA_Q_W = NSA_HEADS * HEAD_DIM
NSA_KV_W = NSA_KV_HEADS * HEAD_DIM
NSA_GATE_W = 3 * NSA_HEADS
MOBA_W = MOBA_HEADS * HEAD_DIM
SB_W = SB_HEADS * HEAD_DIM

SCALE = HEAD_DIM ** -0.5
LOG2_E = 1.0 / math.log(2.0)
BF16_SUBLANES = 16
V_ROWS = -(-(HEAD_DIM + 1) // BF16_SUBLANES) * BF16_SUBLANES
FLASH_SPLIT = 2
MASK_NEG = -(2.0 ** 60)
SB_EXP_ZERO = -110.0

ROW_TILE = 512
PROJ_CHUNK = 512
NSA_Q_TILE = 128
NSA_FLASH_Q_TILE = 256
NSA_SEL_K_TILE = 512
NSA_WIN_K_TILE = 256
MOBA_FLASH_Q_TILE = 512
MOBA_K_TILE = 512
MOBA_AUG = 64
SB_TILE = 256
SB_HEADS_PER_STEP = 4


def _cparams(*sem):
    return pltpu.CompilerParams(dimension_semantics=sem, vmem_limit_bytes=VMEM_LIMIT_BYTES)


def _dot(a, b):
    return jnp.dot(a, b, preferred_element_type=F32)


def _split2(x):
    hi = x.astype(BF16)
    lo = (x - hi.astype(F32)).astype(BF16)
    return hi, lo


def _rope_table_kernel(pos_ref, inv_ref, cos_ref, sin_ref):
    ang = pos_ref[...] * inv_ref[...]
    lane = lax.broadcasted_iota(jnp.int32, ang.shape, 1)
    sign = jnp.where((lane % HEAD_DIM) < HEAD_DIM // 2, -1.0, 1.0)
    cos_ref[...] = jnp.cos(ang)
    sin_ref[...] = jnp.sin(ang) * sign


def _rope_tables(pos_f32):
    T = pos_f32.shape[0]
    half = HEAD_DIM // 2
    inv = ROPE_THETA ** (-np.arange(half, dtype=np.float64) / half)
    inv_row = jnp.asarray(np.tile(inv, LANES // half)[None, :], F32)
    tm = ROW_TILE
    return pl.pallas_call(
        _rope_table_kernel,
        grid=(T // tm,),
        in_specs=[pl.BlockSpec((tm, 1), lambda i: (i, 0)),
                  pl.BlockSpec((1, LANES), lambda i: (0, 0))],
        out_specs=[pl.BlockSpec((tm, LANES), lambda i: (i, 0))] * 2,
        out_shape=[jax.ShapeDtypeStruct((T, LANES), F32)] * 2,
        compiler_params=_cparams("parallel"),
    )(pos_f32, inv_row)


def _proj_heads_kernel(*refs, plan, tm, tiles_per_seq, with_rope):
    if with_rope:
        x_ref, w_ref, cos_ref, sin_ref = refs[:4]
        out_refs = refs[4:]
        cos = cos_ref[...]
        sin = sin_ref[...]
        lane = lax.broadcasted_iota(jnp.int32, cos.shape, 1)
        first_half = (lane % HEAD_DIM) < HEAD_DIM // 2
    else:
        x_ref, w_ref = refs[:2]
        out_refs = refs[2:]
    xb = x_ref[...].astype(BF16)
    pos = (pl.program_id(0) % tiles_per_seq) * tm + lax.broadcasted_iota(jnp.int32, (tm, 1), 0)
    chunks = {}

    def pair(c0, rope):
        k = c0 // PROJ_CHUNK
        if k not in chunks:
            lo = k * PROJ_CHUNK
            chunks[k] = _dot(xb, w_ref[:, lo:min(lo + PROJ_CHUNK, w_ref.shape[1])])
        a = chunks[k][:, c0 - k * PROJ_CHUNK:c0 - k * PROJ_CHUNK + LANES]
        if rope:
            swapped = jnp.where(first_half, pltpu.roll(a, LANES - HEAD_DIM // 2, 1),
                                pltpu.roll(a, HEAD_DIM // 2, 1))
            a = a * cos + swapped * sin
        return a

    for (kind, col0, n_heads, rope, prm), o_ref in zip(plan, out_refs):
        if kind == "sigmoid":
            o_ref[...] = 1.0 / (1.0 + jnp.exp(-pair(col0, False)))
            continue
        for h in range(0, n_heads, 2):
            a = pair(col0 + h * HEAD_DIM, rope)
            if kind in ("rows", "rows_aug"):
                for d in range(2):
                    o_ref[h + d, :, 0:HEAD_DIM] = a[:, d * HEAD_DIM:(d + 1) * HEAD_DIM].astype(o_ref.dtype)
            else:
                at = a.T
                for d in range(2):
                    head = at[d * HEAD_DIM:(d + 1) * HEAD_DIM, :].astype(o_ref.dtype)
                    if kind == "cols":
                        o_ref[h + d, 0:HEAD_DIM, :] = head
                    else:
                        G, tq = prm
                        grp, g = divmod(h + d, G)
                        if tq <= tm:
                            for u in range(tm // tq):
                                o_ref[grp, u, :, g * tq:(g + 1) * tq] = head[:, u * tq:(u + 1) * tq]
                        else:
                            o_ref[grp, 0, :, :] = head
        if kind == "rows_aug":
            block, off, width = prm
            C = o_ref.shape[2]
            onehot = jnp.where(pos // block == lax.broadcasted_iota(jnp.int32, (tm, width), 1), 1.0, 0.0)
            for h in range(n_heads):
                if off > HEAD_DIM:
                    o_ref[h, :, HEAD_DIM:off] = jnp.zeros((tm, off - HEAD_DIM), o_ref.dtype)
                o_ref[h, :, off:off + width] = onehot.astype(o_ref.dtype)
                if off + width < C:
                    o_ref[h, :, off + width:C] = jnp.zeros((tm, C - off - width), o_ref.dtype)
        if kind == "cols" and o_ref.shape[1] > HEAD_DIM:
            VR = o_ref.shape[1]
            extra = jnp.where(lax.broadcasted_iota(jnp.int32, (VR - HEAD_DIM, tm), 0) == 0, 1.0, 0.0)
            for h in range(n_heads):
                o_ref[h, HEAD_DIM:VR, :] = extra.astype(o_ref.dtype)


def _project_heads(x, w_bf16, plan, B, S, tables=None):
    T, D = x.shape
    N = w_bf16.shape[1]
    tm = ROW_TILE
    tps = S // tm
    in_specs = [pl.BlockSpec((tm, D), lambda i: (i, 0)),
                pl.BlockSpec((D, N), lambda i: (0, 0), pipeline_mode=pl.Buffered(1))]
    args = [x, w_bf16]
    if tables is not None:
        in_specs += [pl.BlockSpec((tm, LANES), lambda i: (i, 0))] * 2
        args += list(tables)
    out_specs, out_shapes = [], []
    for kind, col0, n, rope, prm in plan:
        if kind == "sigmoid":
            out_specs.append(pl.BlockSpec((tm, LANES), lambda i: (i, 0)))
            out_shapes.append(jax.ShapeDtypeStruct((T, LANES), F32))
        elif kind == "rows":
            out_specs.append(pl.BlockSpec((n, tm, HEAD_DIM), lambda i: (i // tps, i % tps, 0)))
            out_shapes.append(jax.ShapeDtypeStruct((B * n, S, HEAD_DIM), BF16))
        elif kind == "rows_aug":
            C = prm[3]
            out_specs.append(pl.BlockSpec((n, tm, C), lambda i: (i // tps, i % tps, 0)))
            out_shapes.append(jax.ShapeDtypeStruct((B * n, S, C), BF16))
        elif kind == "cols":
            VR = prm
            out_specs.append(pl.BlockSpec((n, VR, tm), lambda i: (i // tps, 0, i % tps)))
            out_shapes.append(jax.ShapeDtypeStruct((B * n, VR, S), BF16))
        else:
            G, tq = prm
            ng = n // G
            if tq <= tm:
                out_specs.append(pl.BlockSpec((ng, tm // tq, HEAD_DIM, G * tq), lambda i: (i // tps, i % tps, 0, 0)))
            else:
                r = tq // tm
                out_specs.append(pl.BlockSpec((ng, 1, HEAD_DIM, tm),
                                              lambda i, r=r: (i // tps, (i % tps) // r, 0, (i % tps) % r)))
            out_shapes.append(jax.ShapeDtypeStruct((B * ng, S // tq, HEAD_DIM, G * tq), BF16))
    kplan = tuple((k, c, n, r, (p[:3] if k == "rows_aug" else p)) for k, c, n, r, p in plan)
    return pl.pallas_call(
        functools.partial(_proj_heads_kernel, plan=kplan, tm=tm, tiles_per_seq=tps, with_rope=tables is not None),
        grid=(T // tm,),
        in_specs=in_specs,
        out_specs=out_specs,
        out_shape=out_shapes,
        compiler_params=_cparams("parallel"),
    )(*args)


def _compress_kernel(kv_ref, pos_ref, w1_ref, w2_ref, o_ref):
    kv = kv_ref[0].astype(F32)
    n = kv.shape[0]
    half = CMP_STRIDE * HEAD_DIM
    first = _dot((kv + pos_ref[0:1, :]).astype(BF16), w1_ref[0:half, :])
    second = _dot((kv + pos_ref[1:2, :]).astype(BF16), w1_ref[half:2 * half, :])
    h = first + pltpu.roll(second, n - 1, 0)
    g = 0.5 * h * (1.0 + jnp.tanh(math.sqrt(2.0 / math.pi) * (h + 0.044715 * (h * h * h))))
    o_ref[0] = _dot(g.astype(BF16), w2_ref[...]).astype(o_ref.dtype)


def _compress(kv, pos_emb, w1, w2):
    NB, S, _ = kv.shape
    n = S // CMP_STRIDE
    half = CMP_STRIDE * HEAD_DIM
    kvr = kv.reshape(NB, n, half)
    pos2 = pos_emb.reshape(2, half)
    return pl.pallas_call(
        _compress_kernel,
        grid=(NB,),
        in_specs=[pl.BlockSpec((1, n, half), lambda b: (b, 0, 0)),
                  pl.BlockSpec((2, half), lambda b: (0, 0)),
                  pl.BlockSpec((2 * half, CMP_HIDDEN), lambda b: (0, 0)),
                  pl.BlockSpec((CMP_HIDDEN, HEAD_DIM), lambda b: (0, 0))],
        out_specs=pl.BlockSpec((1, n, HEAD_DIM), lambda b: (b, 0, 0)),
        out_shape=jax.ShapeDtypeStruct((NB, n, HEAD_DIM), BF16),
        compiler_params=_cparams("parallel"),
    )(kvr, pos2, w1.astype(BF16), w2.astype(BF16))


def _store_heads_token_major(o_ref, out_t, tq, first_head):
    for g in range(0, out_t.shape[1] // tq, 2):
        pair = jnp.concatenate([out_t[:, g * tq:(g + 1) * tq], out_t[:, (g + 1) * tq:(g + 2) * tq]], axis=0)
        c0 = (first_head + g) * HEAD_DIM
        o_ref[:, c0:c0 + 2 * HEAD_DIM] = pair.T.astype(o_ref.dtype)


def _first_max_pick_t(score, idx, height):
    m = jnp.max(score, axis=0, keepdims=True)
    first = jnp.min(jnp.where(score == m, idx, height), axis=0, keepdims=True)
    return idx == first


def _nsa_cmp_kernel(qt_ref, kc_ref, vct_ref, oc_ref, qaug_ref, *, tq, n_slc, top_n):
    G = NSA_GROUP
    R = G * tq
    q0 = pl.program_id(1) * tq
    qt = qt_ref[0, 0]
    kc = kc_ref[0]
    ncp = kc.shape[0]
    s = _dot(kc, qt * SCALE)
    t_row = q0 + (lax.broadcasted_iota(jnp.int32, (1, R), 1) % tq)
    cmp_end = lax.broadcasted_iota(jnp.int32, (ncp, 1), 0) * CMP_STRIDE + (CMP_BLOCK - 1)
    s = jnp.where(cmp_end <= t_row, s, -jnp.inf)
    m = jnp.max(s, axis=0, keepdims=True)
    m = jnp.where(m == -jnp.inf, 0.0, m)
    e = jnp.exp(s - m)
    p = e * (1.0 / jnp.maximum(jnp.sum(e, axis=0, keepdims=True), 1e-30))
    _store_heads_token_major(oc_ref, _dot(vct_ref[0], p.astype(BF16)), tq, 0)

    pg = p[:, 0:tq]
    for g in range(1, G):
        pg = pg + p[:, g * tq:(g + 1) * tq]
    sj = lax.broadcasted_iota(jnp.int32, (n_slc, ncp), 0)
    ci = lax.broadcasted_iota(jnp.int32, (n_slc, ncp), 1)
    overlap = jnp.where(ci * CMP_STRIDE < (sj + 1) * SLC_BLOCK,
                        jnp.where(ci * CMP_STRIDE + CMP_BLOCK - 1 >= sj * SLC_BLOCK, 1.0, 0.0), 0.0)
    overlap = overlap.astype(BF16)
    p_hi, p_lo = _split2(pg)
    imp = _dot(overlap, p_hi) + _dot(overlap, p_lo)

    blk = lax.broadcasted_iota(jnp.int32, (n_slc, tq), 0)
    cur = (q0 + lax.broadcasted_iota(jnp.int32, (1, tq), 1)) // SLC_BLOCK
    forced = (blk == 0) | (blk == cur) | (blk == cur - 1)
    valid = blk <= cur
    score = jnp.where(forced, imp + FORCE_BONUS, jnp.where(valid, imp, -1.0))

    def pick(_, carry):
        score, sel = carry
        hit = _first_max_pick_t(score, blk, n_slc)
        return jnp.where(hit, -jnp.inf, score), jnp.where(hit, 1.0, sel)

    _, sel = lax.fori_loop(0, top_n, pick, (score, jnp.zeros((n_slc, tq), F32)))
    neg = jnp.where(valid, jnp.where(sel > 0.0, 0.0, MASK_NEG), MASK_NEG).astype(BF16)

    qaug_ref[0, 0, 0:HEAD_DIM, :] = qt
    qaug_ref[0, 0, HEAD_DIM:2 * HEAD_DIM, :] = jnp.zeros((HEAD_DIM, R), BF16)
    for g in range(G):
        qaug_ref[0, 0, 2 * HEAD_DIM:2 * HEAD_DIM + n_slc, g * tq:(g + 1) * tq] = neg


def _token_major_spec(tq, G, n_tiles, groups_per_batch):
    return pl.BlockSpec((tq, G * HEAD_DIM),
                        lambda b, i: ((b // groups_per_batch) * n_tiles + i, b % groups_per_batch))


def _nsa_compressed(qt, kc, vct, S):
    NB, n_tiles, _, R = qt.shape
    G = NSA_GROUP
    tq = R // G
    ncp = kc.shape[1]
    n_slc = S // SLC_BLOCK
    C = 2 * HEAD_DIM + n_slc
    kern = functools.partial(_nsa_cmp_kernel, tq=tq, n_slc=n_slc, top_n=min(SLC_TOPN, n_slc))
    return pl.pallas_call(
        kern,
        grid=(NB, n_tiles),
        in_specs=[pl.BlockSpec((1, 1, HEAD_DIM, R), lambda b, i: (b, i, 0, 0)),
                  pl.BlockSpec((1, ncp, HEAD_DIM), lambda b, i: (b, 0, 0)),
                  pl.BlockSpec((1, HEAD_DIM, ncp), lambda b, i: (b, 0, 0))],
        out_specs=[_token_major_spec(tq, G, n_tiles, NSA_KV_HEADS),
                   pl.BlockSpec((1, 1, C, R), lambda b, i: (b, i, 0, 0))],
        out_shape=[jax.ShapeDtypeStruct((NB * S // NSA_KV_HEADS, NSA_Q_W), BF16),
                   jax.ShapeDtypeStruct((NB, n_tiles, C, R), BF16)],
        compiler_params=_cparams("parallel", "parallel"),
    )(qt, kc, vct)


def _moba_mask_rows(qt, kmean, q0, top_k):
    n_blk, tq = kmean.shape[0], qt.shape[1]
    gate = _dot(kmean, qt)
    blk = lax.broadcasted_iota(jnp.int32, (n_blk, tq), 0)
    cur = (q0 + lax.broadcasted_iota(jnp.int32, (1, tq), 1)) // MOBA_BLOCK
    past = blk < cur
    gate = jnp.where(past, gate, -jnp.inf)
    taken = jnp.zeros((n_blk, tq), F32)
    for _ in range(top_k):
        hit = _first_max_pick_t(jnp.where(taken > 0.0, -jnp.inf, gate), blk, n_blk)
        taken = jnp.where(hit, 1.0, taken)
    return jnp.where(blk == cur, 0.0, jnp.where(past, jnp.where(taken > 0.0, 0.0, MASK_NEG), MASK_NEG))


def _flash_kernel(*refs, G, tq, tk, window, n_split, kv_heads, gate_top_k):
    if gate_top_k:
        qt_ref, k_ref, vt_ref, km_ref, o_ref = refs[:5]
        scratch = refs[5:]
    else:
        qt_ref, k_ref, vt_ref, o_ref = refs[:4]
        scratch = refs[4:]
    R = G * tq * kv_heads
    W = R // n_split
    VR = vt_ref.shape[1]
    q0 = pl.program_id(1) * tq
    q_scrs, s_scrs0, s_scrs1, p_scrs, acc_scrs = (scratch[i * n_split:(i + 1) * n_split] for i in range(5))
    s_bufs = (s_scrs0, s_scrs1)
    t_all = q0 + (lax.broadcasted_iota(jnp.int32, (1, R), 1) % tq)
    for h in range(n_split):
        q_h = qt_ref[0, 0, :, h * W:(h + 1) * W] if kv_heads == 1 else qt_ref[h, 0]
        q_scaled = (q_h.astype(F32) * (SCALE * LOG2_E)).astype(BF16)
        if gate_top_k:
            n_blk, C = km_ref.shape[1], q_scrs[h].shape[0]
            q_scrs[h][0:HEAD_DIM, :] = q_scaled
            q_scrs[h][HEAD_DIM:HEAD_DIM + n_blk, :] = _moba_mask_rows(q_h, km_ref[h], q0, gate_top_k).astype(BF16)
            q_scrs[h][HEAD_DIM + n_blk:C, :] = jnp.zeros((C - HEAD_DIM - n_blk, W), BF16)
        else:
            q_scrs[h][...] = q_scaled
        acc_scrs[h][...] = jnp.zeros((VR, W), F32)

    def group(h, slot, vt_tile, k0, m, masked):
        s_scr, p_scr, acc_scr = s_bufs[slot][h], p_scrs[h], acc_scrs[h]
        t_row = t_all[:, h * W:(h + 1) * W]

        def load(r0, rows):
            s = s_scr[pl.ds(r0, rows), :]
            if masked:
                kpos = k0 + r0 + lax.broadcasted_iota(jnp.int32, (rows, 1), 0)
                if window is not None:
                    s = jnp.where(t_row - kpos < window, s, -jnp.inf)
                s = jnp.where(kpos <= t_row, s, -jnp.inf)
            return s

        def col_max(i, m8):
            return jnp.maximum(m8, load(pl.multiple_of(i * 8, 8), 8))

        m8 = lax.fori_loop(0, tk // 8, col_max, jnp.full((8, W), -jnp.inf, F32), unroll=True)
        m_new = jnp.maximum(m, jnp.max(m8, axis=0, keepdims=True))
        m_safe = jnp.where(m_new == -jnp.inf, 0.0, m_new)
        alpha = jnp.exp2(m - m_safe)

        def probs(i, carry):
            r0 = pl.multiple_of(i * 16, 16)
            p_scr[pl.ds(r0, 16), :] = jnp.exp2(load(r0, 16) - m_safe).astype(BF16)
            return carry

        lax.fori_loop(0, tk // 16, probs, 0, unroll=True)
        acc_scr[...] = alpha * acc_scr[...] + _dot(vt_tile, p_scr[...])
        return m_new

    def scores(j, slot):
        k0 = pl.multiple_of(j * tk, tk)
        for h in range(n_split):
            k_tile = k_ref[h if kv_heads > 1 else 0, pl.ds(k0, tk), :]
            s_bufs[slot][h][...] = _dot(k_tile, q_scrs[h][...])

    def softmax_pv(j, slot, ms, masked):
        k0 = pl.multiple_of(j * tk, tk)
        return tuple(group(h, slot, vt_ref[h if kv_heads > 1 else 0, :, pl.ds(k0, tk)], k0, ms[h], masked)
                     for h in range(n_split))

    def tile(j, ms):
        scores(j, 0)
        return softmax_pv(j, 0, ms, True)

    ms = tuple(jnp.full((1, W), -jnp.inf, F32) for _ in range(n_split))
    j_hi = (q0 + tq - 1) // tk
    if window is None:
        n_pairs = (q0 // tk) // 2

        def pair(i, ms):
            scores(2 * i + 1, 1)
            ms = softmax_pv(2 * i, 0, ms, False)
            scores(2 * i + 2, 0)
            return softmax_pv(2 * i + 1, 1, ms, False)

        scores(0, 0)
        ms = lax.fori_loop(0, n_pairs, pair, ms)
        leftover = j_hi - 2 * n_pairs

        @pl.when(leftover == 0)
        def _():
            softmax_pv(j_hi, 0, ms, True)

        @pl.when(leftover == 1)
        def _():
            scores(j_hi, 1)
            softmax_pv(j_hi, 1, softmax_pv(j_hi - 1, 0, ms, False), True)
    elif window == 2 * tk and tq == tk:
        @pl.when(q0 >= window)
        def _():
            scores(j_hi - 2, 0)
            scores(j_hi - 1, 1)
            m1 = softmax_pv(j_hi - 2, 0, ms, True)
            scores(j_hi, 0)
            softmax_pv(j_hi, 0, softmax_pv(j_hi - 1, 1, m1, False), True)

        @pl.when(q0 < window)
        def _():
            lax.fori_loop(0, j_hi + 1, tile, ms)
    else:
        j_lo = jnp.maximum(q0 - (window - 1), 0) // tk
        lax.fori_loop(j_lo, j_hi + 1, tile, ms)
    outs = []
    for h in range(n_split):
        acc = acc_scrs[h][...]
        outs.append(acc[0:HEAD_DIM, :] / acc[HEAD_DIM:HEAD_DIM + 1, :])
    _store_heads_token_major(o_ref, jnp.concatenate(outs, axis=1), tq, 0)


def _flash(qt, k, vt, *, G, kv_heads, heads_per_batch, tk, window=None, kmean=None):
    NB, n_tiles, Cq, GW = qt.shape
    S, C = k.shape[1], k.shape[2]
    tq = GW // G
    tk = min(tk, S)
    ns = FLASH_SPLIT
    assert kv_heads in (1, ns) and (kv_heads == 1 or G == 1) and tq <= tk
    W = GW * kv_heads // ns
    per_step = G * kv_heads
    out_spec = _token_major_spec(tq, per_step, n_tiles, heads_per_batch // per_step)
    out_shape = jax.ShapeDtypeStruct((NB * G // heads_per_batch * S, heads_per_batch * HEAD_DIM), BF16)
    in_specs = [pl.BlockSpec((kv_heads, 1, Cq, GW), lambda b, i: (b, i, 0, 0)),
                pl.BlockSpec((kv_heads, S, C), lambda b, i: (b, 0, 0)),
                pl.BlockSpec((kv_heads, V_ROWS, S), lambda b, i: (b, 0, 0))]
    args = [qt, k, vt]
    gate_top_k = 0
    if kmean is not None:
        assert kv_heads > 1 and Cq == HEAD_DIM
        n_blk = kmean.shape[1]
        gate_top_k = min(MOBA_TOPK, n_blk)
        in_specs.append(pl.BlockSpec((kv_heads, n_blk, HEAD_DIM), lambda b, i: (b, 0, 0)))
        args.append(kmean)
    kern = functools.partial(_flash_kernel, G=G, tq=tq, tk=tk, window=window, n_split=ns, kv_heads=kv_heads,
                             gate_top_k=gate_top_k)
    return pl.pallas_call(
        kern,
        grid=(NB // kv_heads, S // tq),
        in_specs=in_specs,
        out_specs=out_spec,
        out_shape=out_shape,
        scratch_shapes=([pltpu.VMEM((C, W), BF16)] * ns + [pltpu.VMEM((tk, W), F32)] * (2 * ns)
                        + [pltpu.VMEM((tk, W), BF16)] * ns + [pltpu.VMEM((V_ROWS, W), F32)] * ns),
        compiler_params=_cparams("parallel", "parallel"),
    )(*args)


def _block_mean_kernel(k_ref, o_ref, *, n_blk):
    k = k_ref[0][:, 0:HEAD_DIM].astype(F32).reshape(n_blk, MOBA_BLOCK, HEAD_DIM)
    o_ref[0] = (jnp.sum(k, axis=1) * (1.0 / MOBA_BLOCK)).astype(o_ref.dtype)


def _block_mean(k):
    NB, S, C = k.shape
    n_blk = S // MOBA_BLOCK
    return pl.pallas_call(
        functools.partial(_block_mean_kernel, n_blk=n_blk),
        grid=(NB,),
        in_specs=[pl.BlockSpec((1, S, C), lambda b: (b, 0, 0))],
        out_specs=pl.BlockSpec((1, n_blk, HEAD_DIM), lambda b: (b, 0, 0)),
        out_shape=jax.ShapeDtypeStruct((NB, n_blk, HEAD_DIM), BF16),
        compiler_params=_cparams("parallel"),
    )(k)


def _sb_kernel(qt_ref, k_ref, vt_ref, o_ref, *acc_scrs, tile, hb):
    qi = pl.program_id(1)
    q0 = qi * tile
    t_row = q0 + lax.broadcasted_iota(jnp.int32, (1, tile), 1)
    later = (lax.broadcasted_iota(jnp.int32, (tile, tile), 1)
             > lax.broadcasted_iota(jnp.int32, (tile, tile), 0))
    later = jnp.where(later, 1.0, 0.0).astype(BF16)
    for acc in acc_scrs:
        acc[...] = jnp.zeros((HEAD_DIM, tile), F32)

    def step(j, carries, masked):
        k0 = pl.multiple_of(j * tile, tile)
        zs = [_dot(k_ref[h, pl.ds(k0, tile), :], qt_ref[h, 0] * SCALE) for h in range(hb)]
        if masked:
            mask = (k0 + lax.broadcasted_iota(jnp.int32, (tile, 1), 0)) < t_row
        sps = [jnp.maximum(z, 0.0) + jnp.log(1.0 + jnp.exp(-jnp.abs(z))) for z in zs]
        log_1ms = [jnp.where(mask, -sp, 0.0) if masked else -sp for sp in sps]
        parts = [_split2(x) for x in log_1ms]
        betweens = [_dot(later, hi) + _dot(later, lo) + c for (hi, lo), c in zip(parts, carries)]
        new_carries = []
        for h in range(hb):
            w = jnp.exp((zs[h] - sps[h]) + betweens[h])
            if masked:
                w = jnp.where(mask, w, 0.0)
            acc_scrs[h][...] += _dot(vt_ref[h, :, pl.ds(k0, tile)], w.astype(BF16))
            new_carries.append(carries[h] + jnp.sum(log_1ms[h], axis=0, keepdims=True))
        worst = new_carries[0]
        for c in new_carries[1:]:
            worst = jnp.maximum(worst, c)
        return tuple(new_carries), jnp.max(worst)

    carries, worst = step(qi, tuple(jnp.zeros((1, tile), F32) for _ in range(hb)), True)

    def cond(state):
        j, _, worst = state
        return (j >= 0) & (worst > SB_EXP_ZERO)

    def body(state):
        j, carries, _ = state
        carries, worst = step(j, carries, False)
        return j - 1, carries, worst

    lax.while_loop(cond, body, (qi - 1, carries, worst))
    for h in range(0, hb, 2):
        pair = jnp.concatenate([acc_scrs[h][...], acc_scrs[h + 1][...]], axis=0)
        o_ref[:, h * HEAD_DIM:(h + 2) * HEAD_DIM] = pair.T.astype(o_ref.dtype)


def _stick_breaking(qt, k, vt, n_heads):
    NB, S, _ = k.shape
    tile = qt.shape[3]
    hb = SB_HEADS_PER_STEP
    steps_per_batch = n_heads // hb
    return pl.pallas_call(
        functools.partial(_sb_kernel, tile=tile, hb=hb),
        grid=(NB // hb, S // tile),
        in_specs=[pl.BlockSpec((hb, 1, HEAD_DIM, tile), lambda b, i: (b, i, 0, 0)),
                  pl.BlockSpec((hb, S, HEAD_DIM), lambda b, i: (b, 0, 0)),
                  pl.BlockSpec((hb, HEAD_DIM, S), lambda b, i: (b, 0, 0))],
        out_specs=pl.BlockSpec((tile, hb * HEAD_DIM),
                               lambda b, i: ((b // steps_per_batch) * (S // tile) + i, b % steps_per_batch)),
        out_shape=jax.ShapeDtypeStruct((NB // n_heads * S, n_heads * HEAD_DIM), BF16),
        scratch_shapes=[pltpu.VMEM((HEAD_DIM, tile), F32)] * hb,
        compiler_params=_cparams("parallel", "parallel"),
    )(qt, k, vt)


def _layer_norm(y, g, b):
    mu = jnp.mean(y, axis=-1, keepdims=True)
    d = y - mu
    var = jnp.mean(d * d, axis=-1, keepdims=True)
    return d * lax.rsqrt(var + LN_EPS) * g + b


def _out_ln_kernel(*refs, n_branch, n_plain):
    it = iter(refs)
    x_ref = next(it)
    br_refs = [next(it) for _ in range(n_branch)]
    gate_ref = next(it) if n_branch else None
    o_refs = [next(it) for _ in range(n_plain)]
    w_refs = [next(it) for _ in range(n_plain + (1 if n_branch else 0))]
    g_ref, b_ref, y_ref = next(it), next(it), next(it)
    mix = None
    if n_branch:
        mixed_scr = next(it)
        gate = gate_ref[...]
        for h in range(NSA_HEADS):
            sl = slice(h * HEAD_DIM, (h + 1) * HEAD_DIM)
            o = gate[:, n_branch * h:n_branch * h + 1] * br_refs[0][:, sl]
            for c in range(1, n_branch):
                o = o + gate[:, n_branch * h + c:n_branch * h + c + 1] * br_refs[c][:, sl]
            mixed_scr[:, sl] = o.astype(BF16)
        mix = _dot(mixed_scr[...], w_refs[0][...])
        w_refs = w_refs[1:]
    for o_ref, w_ref in zip(o_refs, w_refs):
        part = _dot(o_ref[...].astype(BF16), w_ref[...])
        mix = part if mix is None else mix + part
    y_ref[...] = _layer_norm(DEEPNORM_ALPHA * x_ref[...] + mix, g_ref[...], b_ref[...])


def _out_ln(x, outs, ws, g, b, nsa=None):
    T, D = x.shape
    tm = ROW_TILE
    row = lambda a: pl.BlockSpec((tm, a.shape[1]), lambda i: (i, 0))
    const = lambda a: pl.BlockSpec(a.shape, lambda i: (0, 0), pipeline_mode=pl.Buffered(1))
    args, in_specs, scratch, n_branch = [x], [row(x)], [], 0
    ws = list(ws)
    if nsa is not None:
        branches, gates, w_nsa = nsa
        n_branch = len(branches)
        args += list(branches) + [gates]
        in_specs += [row(a) for a in branches] + [row(gates)]
        scratch = [pltpu.VMEM((tm, NSA_Q_W), BF16)]
        ws = [w_nsa] + ws
    gb = [g.reshape(1, D), b.reshape(1, D)]
    args += list(outs) + ws + gb
    in_specs += [row(o) for o in outs] + [const(w) for w in ws] + [const(a) for a in gb]
    return pl.pallas_call(
        functools.partial(_out_ln_kernel, n_branch=n_branch, n_plain=len(outs)),
        grid=(T // tm,),
        in_specs=in_specs,
        out_specs=pl.BlockSpec((tm, D), lambda i: (i, 0)),
        out_shape=jax.ShapeDtypeStruct((T, D), F32),
        scratch_shapes=scratch,
        compiler_params=_cparams("parallel"),
    )(*args)


def _route(x, wr_ref, br_ref):
    x1, x2 = _split2(x)
    w1, w2 = wr_ref[0], wr_ref[1]
    logits = _dot(x2, w1) + _dot(x1, w2) + _dot(x1, w1) + br_ref[...]
    lane = lax.broadcasted_iota(jnp.int32, logits.shape, 1)
    is_grp = lane < N_GROUPS
    lg = jnp.where(is_grp, logits, -jnp.inf)
    mg = jnp.max(lg, axis=-1, keepdims=True)
    gidx = jnp.min(jnp.where(lg == mg, lane, LANES), axis=-1, keepdims=True)
    w_g = 1.0 / jnp.sum(jnp.where(is_grp, jnp.exp(logits - mg), 0.0), axis=-1, keepdims=True)
    first = N_GROUPS + gidx * EXPERTS_PER_GROUP
    in_grp = (lane >= first) & (lane < first + EXPERTS_PER_GROUP)
    le = jnp.where(in_grp, logits, -jnp.inf)
    v1 = jnp.max(le, axis=-1, keepdims=True)
    i1 = jnp.min(jnp.where(le == v1, lane, LANES), axis=-1, keepdims=True)
    le2 = jnp.where(lane == i1, -jnp.inf, le)
    v2 = jnp.max(le2, axis=-1, keepdims=True)
    i2 = jnp.min(jnp.where(le2 == v2, lane, LANES), axis=-1, keepdims=True)
    e2 = jnp.exp(v2 - v1)
    den = 1.0 + e2
    return jnp.where(lane == i1, (1.0 / den) * w_g, jnp.where(lane == i2, (e2 / den) * w_g, 0.0))


def _moe_kernel(x_ref, wr_ref, br_ref, wgu_ref, wd_ref, g_ref, b_ref, y_ref, comb_scr, acc_scr, xb_scr):
    x = x_ref[...]
    comb_scr[...] = _route(x, wr_ref, br_ref)
    xb_scr[...] = x.astype(BF16)
    acc_scr[...] = jnp.zeros_like(acc_scr)
    H = EXPERT_HIDDEN

    def expert(e, carry):
        xb = xb_scr[...]
        comb = comb_scr[...]
        lane = lax.broadcasted_iota(jnp.int32, comb.shape, 1)
        c = jnp.sum(jnp.where(lane == N_GROUPS + e, comb, 0.0), axis=-1, keepdims=True)
        gu = _dot(xb, wgu_ref[e])
        gate = gu[:, 0:H]
        h = (gate * (1.0 / (1.0 + jnp.exp(-gate)))) * gu[:, H:2 * H]
        acc_scr[...] += _dot((h * c).astype(BF16), wd_ref[e])
        return carry

    lax.fori_loop(0, N_EXPERTS, expert, 0)
    y_ref[...] = _layer_norm(DEEPNORM_ALPHA * x_ref[...] + acc_scr[...], g_ref[...], b_ref[...])


def _moe_ln(x, w_grp, b_grp, w_rt, b_rt, w_gate, w_up, w_down, g, b):
    T, D = x.shape
    tm = ROW_TILE
    wr = jnp.concatenate([w_grp, w_rt.transpose(1, 0, 2).reshape(D, N_EXPERTS)], axis=1)
    wr = jnp.pad(wr, ((0, 0), (0, LANES - wr.shape[1])))
    wr2 = jnp.stack(_split2(wr))
    br = jnp.pad(jnp.concatenate([b_grp, b_rt.reshape(N_EXPERTS)]), (0, LANES - N_GROUPS - N_EXPERTS))
    H = EXPERT_HIDDEN
    wgu = jnp.concatenate([w_gate, w_up], axis=2).astype(BF16)
    once = pl.Buffered(1)
    return pl.pallas_call(
        _moe_kernel,
        grid=(T // tm,),
        in_specs=[pl.BlockSpec((tm, D), lambda i: (i, 0)),
                  pl.BlockSpec((2, D, LANES), lambda i: (0, 0, 0), pipeline_mode=once),
                  pl.BlockSpec((1, LANES), lambda i: (0, 0), pipeline_mode=once),
                  pl.BlockSpec((N_EXPERTS, D, 2 * H), lambda i: (0, 0, 0), pipeline_mode=once),
                  pl.BlockSpec((N_EXPERTS, H, D), lambda i: (0, 0, 0), pipeline_mode=once),
                  pl.BlockSpec((1, D), lambda i: (0, 0), pipeline_mode=once),
                  pl.BlockSpec((1, D), lambda i: (0, 0), pipeline_mode=once)],
        out_specs=pl.BlockSpec((tm, D), lambda i: (i, 0)),
        out_shape=jax.ShapeDtypeStruct((T, D), F32),
        scratch_shapes=[pltpu.VMEM((tm, LANES), F32), pltpu.VMEM((tm, D), F32),
                        pltpu.VMEM((tm, D), BF16)],
        compiler_params=_cparams("parallel"),
    )(x, wr2, br.reshape(1, LANES), wgu, w_down.astype(BF16), g.reshape(1, D), b.reshape(1, D))


def _nsa_moba_mixer(x, tables, B, S, w_in, cmp_pos_k, cmp_pos_v, cmp_w1_k, cmp_w2_k, cmp_w1_v, cmp_w2_v):
    KV, G, H = NSA_KV_HEADS, NSA_GROUP, MOBA_HEADS
    widths = [NSA_Q_W] + [NSA_KV_W] * 6 + [NSA_GATE_W, MOBA_W, MOBA_W, MOBA_W]
    qa_w, kc_w, vc_w, ks_w, vs_w, kw_w, vw_w, ga_w, qm_w, km_w, vm_w = jnp.split(
        w_in, [int(v) for v in np.cumsum(widths)[:-1]], axis=1)
    ga_w = jnp.pad(ga_w, ((0, 0), (0, LANES - NSA_GATE_W)))
    parts = [qa_w, kc_w, ks_w, kw_w, qm_w, km_w, vc_w, vs_w, vw_w, vm_w, ga_w]
    col = [0] + [int(v) for v in np.cumsum([p.shape[1] for p in parts])]
    n_slc = S // SLC_BLOCK
    tqn = min(NSA_FLASH_Q_TILE, S)
    tqm = min(MOBA_FLASH_Q_TILE, S)
    plan = [("cols_tiles", col[0], NSA_HEADS, True, (G, tqn)),
            ("rows", col[1], KV, True, None),
            ("rows_aug", col[2], KV, True, (SLC_BLOCK, 2 * HEAD_DIM, n_slc, 2 * HEAD_DIM + n_slc)),
            ("rows", col[3], KV, True, None),
            ("cols_tiles", col[4], H, True, (1, tqm)),
            ("rows_aug", col[5], H, True, (MOBA_BLOCK, HEAD_DIM, MOBA_AUG, HEAD_DIM + MOBA_AUG)),
            ("rows", col[6], KV, False, None),
            ("cols", col[7], KV, False, V_ROWS),
            ("cols", col[8], KV, False, V_ROWS),
            ("cols", col[9], H, False, V_ROWS),
            ("sigmoid", col[10], 0, False, None)]
    (qt, kc, ks_aug, kw, qmt, km_aug, vc, vst, vwt, vmt, gates) = _project_heads(
        x, jnp.concatenate(parts, axis=1).astype(BF16), plan, B, S, tables)

    kcc = _compress(kc, cmp_pos_k, cmp_w1_k, cmp_w2_k)
    vcc = _compress(vc, cmp_pos_v, cmp_w1_v, cmp_w2_v)
    o_c, qt_aug = _nsa_compressed(qt, kcc, vcc.transpose(0, 2, 1), S)
    o_s = _flash(qt_aug, ks_aug, vst, G=G, kv_heads=1, heads_per_batch=NSA_HEADS, tk=NSA_SEL_K_TILE)
    o_w = _flash(qt, kw, vwt, G=G, kv_heads=1, heads_per_batch=NSA_HEADS, tk=NSA_WIN_K_TILE, window=WINDOW)

    o_b = _flash(qmt, km_aug, vmt, G=1, kv_heads=FLASH_SPLIT, heads_per_batch=H, tk=MOBA_K_TILE,
                 kmean=_block_mean(km_aug))
    return (o_c, o_s, o_w), gates, o_b


def _sb_mixer(x, B, S, w_in):
    H = SB_HEADS
    plan = [("cols_tiles", 0, H, False, (1, min(SB_TILE, S))),
            ("rows", SB_W, H, False, None),
            ("cols", 2 * SB_W, H, False, HEAD_DIM)]
    qt, k, vt = _project_heads(x, w_in.astype(BF16), plan, B, S)
    return _stick_breaking(qt, k, vt, H)


def kernel(x, positions, ab_w_in, ab_w_out, nsa_cmp_pos_k, nsa_cmp_pos_v, nsa_cmp_w1_k, nsa_cmp_w2_k,
           nsa_cmp_w1_v, nsa_cmp_w2_v, sb_w_in, sb_w_out, ln_mix_g, ln_mix_b, ln_ffn_g, ln_ffn_b,
           moe_w_grp, moe_b_grp, moe_w_rt, moe_b_rt, moe_w_gate, moe_w_up, moe_w_down):
    B, S, D = x.shape
    T = B * S
    assert S % ROW_TILE == 0 and S % MOBA_BLOCK == 0 and S // MOBA_BLOCK <= MOBA_AUG
    h = x.reshape(T, D)
    tables = _rope_tables(positions.reshape(T, 1).astype(F32))
    n_layers = ln_mix_g.shape[0]
    for layer in range(n_layers):
        i = layer // 2
        if layer % 2 == 0:
            branches, gates, o_b = _nsa_moba_mixer(
                h, tables, B, S, ab_w_in[i], nsa_cmp_pos_k[i], nsa_cmp_pos_v[i],
                nsa_cmp_w1_k[i], nsa_cmp_w2_k[i], nsa_cmp_w1_v[i], nsa_cmp_w2_v[i])
            w_out = ab_w_out[i].astype(BF16)
            h = _out_ln(h, [o_b], [w_out[NSA_Q_W:]], ln_mix_g[layer], ln_mix_b[layer],
                        nsa=(branches, gates, w_out[:NSA_Q_W]))
        else:
            o = _sb_mixer(h, B, S, sb_w_in[i])
            h = _out_ln(h, [o], [sb_w_out[i].astype(BF16)], ln_mix_g[layer], ln_mix_b[layer])
        h = _moe_ln(h, moe_w_grp[layer], moe_b_grp[layer], moe_w_rt[layer], moe_b_rt[layer],
                    moe_w_gate[layer], moe_w_up[layer], moe_w_down[layer], ln_ffn_g[layer], ln_ffn_b[layer])
    return h.reshape(B, S, D)
```
